```python
import math
import jax
import jax.numpy as jnp
from jax import lax
import numpy as np

D_MODEL = 1024
BATCH = 16
SEQ = 4096
DEPTH = 4

PLE_DIM = 256
N_MIXERS = 2
HEAD_DIM = 64
N_HEADS = D_MODEL // HEAD_DIM
ATTN_WIDTH = N_HEADS * HEAD_DIM
WIN_DIL = ((128, 1), (512, 4), (2048, 16))
N_GROUPS = len(WIN_DIL)
ROPE_THETA = 10000.0
CONV_WIDTH = 31
CONV_CH = D_MODEL
RMS_EPS = 1e-6
LN_EPS = 1e-5
NEG_INF = -1e30
ATTN_IN_COLS = 3 * N_GROUPS * ATTN_WIDTH + ATTN_WIDTH
CONV_IN_COLS = 3 * CONV_CH
N_ATTN_LAYERS = (DEPTH + 1) // 2
N_CONV_LAYERS = DEPTH // 2

kernel_name = "hybrid_dilated_attn_conformer_trunk"


def _rmsnorm(x, g):
    xf = x.astype(jnp.float32)
    y = xf * lax.rsqrt(jnp.mean(xf * xf, axis=-1, keepdims=True) + RMS_EPS)
    return (y * g.astype(jnp.float32)).astype(x.dtype)


def _layernorm(x, g, b):
    xf = x.astype(jnp.float32)
    mu = jnp.mean(xf, axis=-1, keepdims=True)
    var = jnp.mean(jnp.square(xf - mu), axis=-1, keepdims=True)
    y = (xf - mu) * lax.rsqrt(var + LN_EPS)
    return (y * g.astype(jnp.float32) + b.astype(jnp.float32)).astype(x.dtype)


def _rope_tables(positions):
    inv_freq = 1.0 / (ROPE_THETA ** (jnp.arange(0, HEAD_DIM, 2, dtype=jnp.float32) / HEAD_DIM))
    ang = positions.astype(jnp.float32)[..., None] * inv_freq
    return jnp.cos(ang)[:, :, None, None, :], jnp.sin(ang)[:, :, None, None, :]


def _apply_rope(t, cos, sin):
    tf = t.astype(jnp.float32)
    t1, t2 = jnp.split(tf, 2, axis=-1)
    out = jnp.concatenate([t1 * cos - t2 * sin, t2 * cos + t1 * sin], axis=-1)
    return out.astype(t.dtype)


def _dilated_window_attention(q, k, v, dilation, n_back):
    B, S, H, E = q.shape
    L = S // dilation
    nb = -(-L // n_back)
    Lp = nb * n_back

    def to_blocks(t):
        t = t.reshape(B, L, dilation, H, E).transpose(0, 2, 1, 3, 4)
        t = jnp.pad(t, ((0, 0), (0, 0), (0, Lp - L), (0, 0), (0, 0)))
        return t.reshape(B, dilation, nb, n_back, H, E)

    def with_prev(t):
        prev = jnp.pad(t, ((0, 0), (0, 0), (1, 0), (0, 0), (0, 0), (0, 0)))[:, :, :-1]
        return jnp.concatenate([prev, t], axis=3)

    qb = to_blocks(q)
    kk = with_prev(to_blocks(k))
    vv = with_prev(to_blocks(v))

    s = jnp.einsum('brnqhe,brnkhe->brnhqk', qb, kk).astype(jnp.float32) * (E ** -0.5)
    qi = jnp.arange(n_back)[:, None]
    kj = jnp.arange(2 * n_back)[None, :]
    dist = n_back + qi - kj
    band = (dist >= 0) & (dist <= n_back)
    has_prev = (jnp.arange(nb) > 0)[:, None, None] | (kj >= n_back)[None]
    valid = band[None] & has_prev
    s = jnp.where(valid[None, None, :, None, :, :], s, NEG_INF)
    lse = jax.nn.logsumexp(s, axis=-1)
    prob = jnp.exp(s - lse[..., None])
    o = jnp.einsum('brnhqk,brnkhe->brnqhe', prob.astype(vv.dtype), vv).astype(jnp.float32)

    o = o.reshape(B, dilation, Lp, H, E)[:, :, :L].transpose(0, 2, 1, 3, 4).reshape(B, S, H, E)
    lse = lse.transpose(0, 1, 2, 4, 3).reshape(B, dilation, Lp, H)[:, :, :L]
    lse = lse.transpose(0, 2, 1, 3).reshape(B, S, H)
    return o, lse


def _attention_branch(h, w_in, w_out, cos, sin):
    B, S, _ = h.shape
    proj = h @ w_in.astype(h.dtype)
    qkv = proj[..., :3 * N_GROUPS * ATTN_WIDTH].reshape(B, S, 3, N_GROUPS, N_HEADS, HEAD_DIM)
    z = proj[..., 3 * N_GROUPS * ATTN_WIDTH:]
    q = _apply_rope(qkv[:, :, 0], cos, sin)
    k = _apply_rope(qkv[:, :, 1], cos, sin)
    v = qkv[:, :, 2]
    outs, lses = [], []
    for g, (window, dilation) in enumerate(WIN_DIL):
        o_g, lse_g = _dilated_window_attention(q[:, :, g], k[:, :, g], v[:, :, g],
                                               dilation, window // dilation)
        outs.append(o_g)
        lses.append(lse_g)
    wts = jax.nn.softmax(jnp.stack(lses, axis=0), axis=0)
    o = jnp.sum(wts[..., None] * jnp.stack(outs, axis=0), axis=0)
    o = o.reshape(B, S, ATTN_WIDTH).astype(h.dtype)
    return (o * jax.nn.silu(z)) @ w_out.astype(h.dtype)


def _conv_branch(h, w_in, dw_w, dw_b, ln_g, ln_b, w_out):
    proj = h @ w_in.astype(h.dtype)
    a, b, z = jnp.split(proj, 3, axis=-1)
    u = a * jax.nn.sigmoid(b)
    u = lax.conv_general_dilated(
        u, dw_w.astype(u.dtype)[:, None, :],
        window_strides=(1,), padding=((CONV_WIDTH - 1, 0),),
        dimension_numbers=('NWC', 'WIO', 'NWC'),
        feature_group_count=CONV_CH) + dw_b.astype(u.dtype)
    u = jax.nn.silu(_layernorm(u, ln_g, ln_b))
    return (u * jax.nn.silu(z)) @ w_out.astype(h.dtype)


def setup_inputs(seed: int = 0) -> dict:
    key = jax.random.key(seed)
    ks = jax.random.split(key, 16)
    nrm = jax.random.normal
    f32 = jnp.float32
    x = nrm(ks[0], (BATCH, SEQ, D_MODEL), f32)
    p = nrm(ks[1], (DEPTH, BATCH, SEQ, PLE_DIM), f32)
    offset = jax.random.randint(ks[2], (BATCH, 1), 0, SEQ, dtype=jnp.int32)
    positions = jnp.arange(SEQ, dtype=jnp.int32)[None, :] + offset
    pre_norm_g = 1.0 + 0.05 * nrm(ks[3], (DEPTH, D_MODEL), f32)
    post_norm_g = 1.0 + 0.05 * nrm(ks[4], (DEPTH, D_MODEL), f32)
    attn_w_in = nrm(ks[5], (N_ATTN_LAYERS, D_MODEL, ATTN_IN_COLS), f32) * D_MODEL ** -0.5
    attn_w_out = nrm(ks[6], (N_ATTN_LAYERS, ATTN_WIDTH, D_MODEL), f32) * ATTN_WIDTH ** -0.5
    conv_w_in = nrm(ks[7], (N_CONV_LAYERS, D_MODEL, CONV_IN_COLS), f32) * D_MODEL ** -0.5
    conv_dw_w = nrm(ks[8], (N_CONV_LAYERS, CONV_WIDTH, CONV_CH), f32) * CONV_WIDTH ** -0.5
    conv_dw_b = 0.02 * nrm(ks[9], (N_CONV_LAYERS, CONV_CH), f32)
    conv_ln_g = 1.0 + 0.05 * nrm(ks[10], (N_CONV_LAYERS, CONV_CH), f32)
    conv_ln_b = 0.02 * nrm(ks[11], (N_CONV_LAYERS, CONV_CH), f32)
    conv_w_out = nrm(ks[12], (N_CONV_LAYERS, CONV_CH, D_MODEL), f32) * CONV_CH ** -0.5
    ple_w_proj = nrm(ks[13], (DEPTH, PLE_DIM, D_MODEL), f32) * PLE_DIM ** -0.5
    ple_w_gate = nrm(ks[14], (DEPTH, D_MODEL, D_MODEL), f32) * D_MODEL ** -0.5
    return {"x": x, "p": p, "positions": positions,
            "pre_norm_g": pre_norm_g, "post_norm_g": post_norm_g,
            "attn_w_in": attn_w_in, "attn_w_out": attn_w_out,
            "conv_w_in": conv_w_in, "conv_dw_w": conv_dw_w, "conv_dw_b": conv_dw_b,
            "conv_ln_g": conv_ln_g, "conv_ln_b": conv_ln_b, "conv_w_out": conv_w_out,
            "ple_w_proj": ple_w_proj, "ple_w_gate": ple_w_gate}


def reference(x, p, positions, pre_norm_g, post_norm_g, attn_w_in, attn_w_out,
              conv_w_in, conv_dw_w, conv_dw_b, conv_ln_g, conv_ln_b, conv_w_out,
              ple_w_proj, ple_w_gate):
    cos, sin = _rope_tables(positions)
    for i in range(DEPTH):
        h = _rmsnorm(x, pre_norm_g[i])
        j = i // N_MIXERS
        if i % N_MIXERS == 0:
            y = _attention_branch(h, attn_w_in[j], attn_w_out[j], cos, sin)
        else:
            y = _conv_branch(h, conv_w_in[j], conv_dw_w[j], conv_dw_b[j],
                             conv_ln_g[j], conv_ln_b[j], conv_w_out[j])
        x = x + _rmsnorm(y, post_norm_g[i])
        x = x + (p[i].astype(x.dtype) @ ple_w_proj[i].astype(x.dtype)) * \
            jax.nn.sigmoid(x @ ple_w_gate[i].astype(x.dtype))
    return x
```

```python
import functools

import jax
import jax.numpy as jnp
import numpy as np
from jax import lax
from jax.experimental import pallas as pl
from jax.experimental.pallas import tpu as pltpu

D_MODEL = 1024
PLE_DIM = 256
HEAD_DIM = 64
N_HEADS = 16
N_PAIRS = N_HEADS // 2
ATTN_WIDTH = N_HEADS * HEAD_DIM
DILATIONS = (1, 4, 16)
N_BACK = 128
N_GROUPS = 3
ROPE_THETA = 10000.0
CONV_WIDTH = 31
CONV_HALO = 32
RMS_EPS = 1e-6
LN_EPS = 1e-5
NEG_INF = -1e30
LANES = 128
QK_SCALE = HEAD_DIM ** -0.5

F32 = jnp.float32
BF16 = jnp.bfloat16

VMEM_LIMIT = 56 * 1024 * 1024


def _const_spec(shape):
    return pl.BlockSpec(shape, lambda *_: (0,) * len(shape), pipeline_mode=pl.Buffered(1))


def _rmsnorm(x, g):
    return x * lax.rsqrt(jnp.mean(x * x, axis=-1, keepdims=True) + RMS_EPS) * g


def _sigmoid(x):
    return 1.0 / (1.0 + jnp.exp(-x))


def _rope_kernel(pos_ref, invf_ref, cos_ref, sin_ref):
    ang = pos_ref[0].astype(F32) * invf_ref[...]
    lane = lax.broadcasted_iota(jnp.int32, ang.shape, 1)
    cos_ref[0] = jnp.cos(ang)
    sin_ref[0] = jnp.where(lane < LANES // 2, -jnp.sin(ang), jnp.sin(ang))


def _rope_tables(positions):
    B, S = positions.shape
    ts = 512
    inv_freq = 1.0 / (ROPE_THETA ** (jnp.arange(0, HEAD_DIM, 2, dtype=F32) / HEAD_DIM))
    invf = jnp.tile(inv_freq, LANES // (HEAD_DIM // 2))[None, :]
    return pl.pallas_call(
        _rope_kernel,
        grid=(B, S // ts),
        in_specs=[pl.BlockSpec((1, ts, 1), lambda b, i: (b, i, 0)),
                  pl.BlockSpec((1, LANES), lambda b, i: (0, 0))],
        out_specs=[pl.BlockSpec((1, ts, LANES), lambda b, i: (b, i, 0))] * 2,
        out_shape=[jax.ShapeDtypeStruct((B, S, LANES), F32)] * 2,
        name="rope_tables",
    )(positions[:, :, None], invf)


def _deinterleave(ref, lead, d, rows):
    parts = [ref[lead + (pl.ds(r, rows // d, stride=d), slice(None))] for r in range(d)]
    return jnp.concatenate(parts, axis=0)


def _attn_in_kernel(x_ref, cos_ref, sin_ref, g_ref, w_ref, *refs, tm):
    out_refs = refs[:10]
    hs_ref = refs[10]
    qkv_refs = [out_refs[3 * g:3 * g + 3] for g in range(N_GROUPS)]
    z_ref = out_refs[9]

    hn = _rmsnorm(x_ref[0], g_ref[...])
    for c in range(N_PAIRS):
        hs_ref[c] = hn[:, c * LANES:(c + 1) * LANES]
    h0 = hn.astype(BF16)

    for g, d in enumerate(DILATIONS):
        if d == 1:
            hg, cg, sg = h0, cos_ref[0], sin_ref[0]
        else:
            hg = jnp.concatenate(
                [_deinterleave(hs_ref, (c,), d, tm) for c in range(N_PAIRS)], axis=1).astype(BF16)
            cg = _deinterleave(cos_ref, (0,), d, tm)
            sg = _deinterleave(sin_ref, (0,), d, tm)
        for part in range(3):
            col0 = (part * N_GROUPS + g) * ATTN_WIDTH
            for cc in range(ATTN_WIDTH // 256):
                acc = jnp.dot(hg, w_ref[:, col0 + cc * 256:col0 + (cc + 1) * 256],
                              preferred_element_type=F32)
                for half in range(2):
                    t = acc[:, half * LANES:(half + 1) * LANES]
                    if part < 2:
                        t = t * cg + pltpu.roll(t, LANES // 2, axis=1) * sg
                    if part == 0:
                        t = t * QK_SCALE
                    qkv_refs[g][part][0, cc * 2 + half] = (
                        t.reshape(d, tm // d, LANES).astype(BF16))
    zc0 = 3 * N_GROUPS * ATTN_WIDTH
    for cc in range(ATTN_WIDTH // 256):
        acc = jnp.dot(h0, w_ref[:, zc0 + cc * 256:zc0 + (cc + 1) * 256],
                      preferred_element_type=F32)
        for half in range(2):
            z_ref[0, cc * 2 + half] = acc[:, half * LANES:(half + 1) * LANES].astype(BF16)


def _attn_in_proj(x, cos, sin, g, w):
    B, S, _ = x.shape
    tm = 256
    ncols = w.shape[1]
    out_shape, out_specs = [], []
    for d in DILATIONS:
        L = S // d
        for _ in range(3):
            out_shape.append(jax.ShapeDtypeStruct((B, N_PAIRS, d, L, LANES), BF16))
            out_specs.append(pl.BlockSpec((1, N_PAIRS, d, tm // d, LANES),
                                          lambda b, i: (b, 0, 0, i, 0)))
    out_shape.append(jax.ShapeDtypeStruct((B, N_PAIRS, S, LANES), BF16))
    out_specs.append(pl.BlockSpec((1, N_PAIRS, tm, LANES), lambda b, i: (b, 0, i, 0)))
    return pl.pallas_call(
        functools.partial(_attn_in_kernel, tm=tm),
        grid=(B, S // tm),
        in_specs=[pl.BlockSpec((1, tm, D_MODEL), lambda b, i: (b, i, 0)),
                  pl.BlockSpec((1, tm, LANES), lambda b, i: (b, i, 0)),
                  pl.BlockSpec((1, tm, LANES), lambda b, i: (b, i, 0)),
                  _const_spec((1, D_MODEL)),
                  _const_spec((D_MODEL, ncols))],
        out_specs=out_specs,
        out_shape=out_shape,
        scratch_shapes=[pltpu.VMEM((N_PAIRS, tm, LANES), F32)],
        compiler_params=pltpu.CompilerParams(
            dimension_semantics=("parallel", "parallel"), vmem_limit_bytes=VMEM_LIMIT),
        name="attn_in_proj",
    )(x, cos, sin, g, w)


def _attn_kernel(*refs, S):
    qkv_refs = [refs[3 * g:3 * g + 3] for g in range(N_GROUPS)]
    z_ref, o_ref, acc_scr, m_scr, l_scr = refs[9:14]
    T = N_BACK

    lane_q = lax.broadcasted_iota(jnp.int32, (T, LANES), 1)
    q_is_h0 = (lane_q // 32) % 2 == 0
    v_is_h0 = lane_q < HEAD_DIM
    diff = (lax.broadcasted_iota(jnp.int32, (T, 2 * T), 0)
            - lax.broadcasted_iota(jnp.int32, (T, 2 * T), 1))

    for g, d in enumerate(DILATIONS):
        q_ref, k_ref, v_ref = qkv_refs[g]
        nb = (S // d) // T

        def block(idx, carry, g=g, d=d, nb=nb, q_ref=q_ref, k_ref=k_ref, v_ref=v_ref):
            r = idx // nb
            n = idx % nb
            start = pl.multiple_of(n * T, T)
            kstart = pl.multiple_of(jnp.maximum(start - T, 0), T)
            dist = diff + (start - kstart)
            valid = (dist >= 0) & (dist <= T)
            q = q_ref[0, 0, r, pl.ds(start, T), :]
            k = k_ref[0, 0, r, pl.ds(kstart, 2 * T), :]
            v = v_ref[0, 0, r, pl.ds(kstart, 2 * T), :]
            pv, mm, ll = [], [], []
            for h in range(2):
                qh = jnp.where(q_is_h0 if h == 0 else ~q_is_h0, q, jnp.zeros_like(q))
                s = lax.dot_general(qh, k, (((1,), (1,)), ((), ())),
                                    preferred_element_type=F32)
                s = jnp.where(valid, s, NEG_INF)
                m = jnp.max(s, axis=-1, keepdims=True)
                p = jnp.exp(s - m)
                ll.append(jnp.sum(p, axis=-1, keepdims=True))
                mm.append(m)
                pv.append(jnp.dot(p.astype(BF16), v, preferred_element_type=F32))
            acc = jnp.where(v_is_h0, pv[0], pv[1])
            mb = jnp.where(v_is_h0, mm[0], mm[1])
            lb = jnp.where(v_is_h0, ll[0], ll[1])
            if d == 1:
                rows = pl.ds(start, T)
            else:
                rows = pl.ds(start * d + r, T, stride=d)
            acc_scr[g, rows, :] = acc
            m_scr[g, rows, :] = mb
            l_scr[g, rows, :] = lb
            return carry

        lax.fori_loop(0, d * nb, block, 0)

    def combine(c, carry):
        rows = pl.ds(pl.multiple_of(c * T, T), T)
        ms = [m_scr[g, rows, :] for g in range(N_GROUPS)]
        mx = jnp.maximum(jnp.maximum(ms[0], ms[1]), ms[2])
        num = jnp.zeros((T, LANES), F32)
        den = jnp.zeros((T, LANES), F32)
        for g in range(N_GROUPS):
            w = jnp.exp(ms[g] - mx)
            num = num + w * acc_scr[g, rows, :]
            den = den + w * l_scr[g, rows, :]
        zf = z_ref[0, 0, rows, :].astype(F32)
        o_ref[0, 0, rows, :] = ((num / den) * (zf * _sigmoid(zf))).astype(BF16)
        return carry

    lax.fori_loop(0, S // T, combine, 0)


def _attention(qkv, z):
    B, _, S, _ = z.shape
    in_specs = []
    for d in DILATIONS:
        L = S // d
        in_specs += [pl.BlockSpec((1, 1, d, L, LANES), lambda b, p: (b, p, 0, 0, 0))] * 3
    in_specs.append(pl.BlockSpec((1, 1, S, LANES), lambda b, p: (b, p, 0, 0)))
    return pl.pallas_call(
        functools.partial(_attn_kernel, S=S),
        grid=(B, N_PAIRS),
        in_specs=in_specs,
        out_specs=pl.BlockSpec((1, 1, S, LANES), lambda b, p: (b, p, 0, 0)),
        out_shape=jax.ShapeDtypeStruct((B, N_PAIRS, S, LANES), BF16),
        scratch_shapes=[pltpu.VMEM((N_GROUPS, S, LANES), F32)] * 3,
        compiler_params=pltpu.CompilerParams(
            dimension_semantics=("parallel", "parallel"), vmem_limit_bytes=VMEM_LIMIT),
        name="dilated_attention",
    )(*qkv, z)


def _layer_tail(y, x, p_bf16, post_g, w_proj_ref, w_gate_ref):
    x1 = x + _rmsnorm(y, post_g)
    gate = _sigmoid(jnp.dot(x1.astype(BF16), w_gate_ref[...], preferred_element_type=F32))
    pe = jnp.dot(p_bf16, w_proj_ref[...], preferred_element_type=F32)
    return x1 + pe * gate


def _attn_out_kernel(o_ref, x_ref, p_ref, g_ref, wo_ref, wp_ref, wg_ref, out_ref):
    o = jnp.concatenate([o_ref[0, c] for c in range(N_PAIRS)], axis=1)
    y = jnp.dot(o, wo_ref[...], preferred_element_type=F32)
    out_ref[0] = _layer_tail(y, x_ref[0], p_ref[0, 0].astype(BF16), g_ref[...], wp_ref, wg_ref)


def _attn_out_proj(og, x, p, layer, post_g, w_out, w_proj, w_gate):
    B, S, _ = x.shape
    tm = 512
    return pl.pallas_call(
        _attn_out_kernel,
        grid=(B, S // tm),
        in_specs=[pl.BlockSpec((1, N_PAIRS, tm, LANES), lambda b, i: (b, 0, i, 0)),
                  pl.BlockSpec((1, tm, D_MODEL), lambda b, i: (b, i, 0)),
                  pl.BlockSpec((1, 1, tm, PLE_DIM), lambda b, i: (layer, b, i, 0)),
                  _const_spec((1, D_MODEL)),
                  _const_spec((ATTN_WIDTH, D_MODEL)),
                  _const_spec((PLE_DIM, D_MODEL)),
                  _const_spec((D_MODEL, D_MODEL))],
        out_specs=pl.BlockSpec((1, tm, D_MODEL), lambda b, i: (b, i, 0)),
        out_shape=jax.ShapeDtypeStruct((B, S, D_MODEL), F32),
        compiler_params=pltpu.CompilerParams(
            dimension_semantics=("parallel", "parallel"), vmem_limit_bytes=VMEM_LIMIT),
        name="attn_out_proj",
    )(og, x, p, post_g, w_out, w_proj, w_gate)


def _conv_kernel(xh_ref, x_ref, p_ref, pre_g_ref, w_in_ref, dw_w_ref, dw_b_ref, ln_g_ref,
                 ln_b_ref, w_out_ref, post_g_ref, wp_ref, wg_ref, out_ref, u_scr, cv_scr, *, tm):
    C = D_MODEL
    x = x_ref[0]
    xa = jnp.concatenate([xh_ref[0], x], axis=0)
    h = _rmsnorm(xa, pre_g_ref[...]).astype(BF16)
    ab = jnp.dot(h, w_in_ref[:, :2 * C], preferred_element_type=F32)
    u = ab[:, :C] * _sigmoid(ab[:, C:])
    row = lax.broadcasted_iota(jnp.int32, u.shape, 0)
    u = jnp.where((row >= CONV_HALO) | (pl.program_id(1) > 0), u, 0.0)
    for c in range(C // LANES):
        u_scr[c] = u[:, c * LANES:(c + 1) * LANES]
    z = jnp.dot(h[CONV_HALO:], w_in_ref[:, 2 * C:], preferred_element_type=F32)

    RB = 128
    first = CONV_HALO - (CONV_WIDTH - 1)

    def conv_rows(i, carry):
        r0 = pl.multiple_of(i * RB, RB)
        for c in range(C // LANES):
            lanes = slice(c * LANES, (c + 1) * LANES)
            win = u_scr.at[c, pl.ds(r0, RB + CONV_HALO), :]
            acc = jnp.broadcast_to(dw_b_ref[:, lanes], (RB, LANES))
            for w in range(CONV_WIDTH):
                acc = acc + win[pl.ds(first + w, RB), :] * dw_w_ref[pl.ds(w, 1), lanes]
            cv_scr[c, pl.ds(r0, RB), :] = acc
        return carry

    lax.fori_loop(0, tm // RB, conv_rows, 0)

    cv = jnp.concatenate([cv_scr[c] for c in range(C // LANES)], axis=1)
    mu = jnp.mean(cv, axis=-1, keepdims=True)
    cen = cv - mu
    var = jnp.mean(cen * cen, axis=-1, keepdims=True)
    y = cen * lax.rsqrt(var + LN_EPS) * ln_g_ref[...] + ln_b_ref[...]
    y = y * _sigmoid(y)
    yg = (y * (z * _sigmoid(z))).astype(BF16)
    out = jnp.dot(yg, w_out_ref[...], preferred_element_type=F32)
    out_ref[0] = _layer_tail(out, x, p_ref[0, 0].astype(BF16), post_g_ref[...], wp_ref, wg_ref)


def _conv_layer(x, p, layer, pre_g, w_in, dw_w, dw_b, ln_g, ln_b, w_out, post_g, w_proj, w_gate):
    B, S, C = x.shape
    tm = 512
    hb = tm // CONV_HALO
    return pl.pallas_call(
        functools.partial(_conv_kernel, tm=tm),
        grid=(B, S // tm),
        in_specs=[pl.BlockSpec((1, CONV_HALO, C), lambda b, i: (b, jnp.maximum(i * hb - 1, 0), 0)),
                  pl.BlockSpec((1, tm, C), lambda b, i: (b, i, 0)),
                  pl.BlockSpec((1, 1, tm, PLE_DIM), lambda b, i: (layer, b, i, 0)),
                  _const_spec((1, C)),
                  _const_spec((C, 3 * C)),
                  _const_spec((CONV_WIDTH, C)),
                  _const_spec((1, C)),
                  _const_spec((1, C)),
                  _const_spec((1, C)),
                  _const_spec((C, C)),
                  _const_spec((1, C)),
                  _const_spec((PLE_DIM, C)),
                  _const_spec((C, C))],
        out_specs=pl.BlockSpec((1, tm, C), lambda b, i: (b, i, 0)),
        out_shape=jax.ShapeDtypeStruct((B, S, C), F32),
        scratch_shapes=[pltpu.VMEM((C // LANES, tm + CONV_HALO, LANES), F32),
                        pltpu.VMEM((C // LANES, tm, LANES), F32)],
        compiler_params=pltpu.CompilerParams(
            dimension_semantics=("parallel", "parallel"), vmem_limit_bytes=VMEM_LIMIT),
        name="conv_layer",
    )(x, x, p, pre_g, w_in, dw_w, dw_b, ln_g, ln_b, w_out, post_g, w_proj, w_gate)


def _qk_column_permutation():
    cols = np.arange(3 * N_GROUPS * ATTN_WIDTH + ATTN_WIDTH)
    lane = np.arange(LANES)
    within = ((lane // 32) % 2) * HEAD_DIM + (lane // 64) * 32 + lane % 32
    for blk in range(2 * N_GROUPS * N_PAIRS):
        cols[blk * LANES:(blk + 1) * LANES] = blk * LANES + within
    return cols


def kernel(x, p, positions, pre_norm_g, post_norm_g, attn_w_in, attn_w_out, conv_w_in, conv_dw_w,
           conv_dw_b, conv_ln_g, conv_ln_b, conv_w_out, ple_w_proj, ple_w_gate):
    depth = p.shape[0]
    cos, sin = _rope_tables(positions)
    perm = _qk_column_permutation()
    for i in range(depth):
        j = i // 2
        pre_g = pre_norm_g[i][None, :]
        post_g = post_norm_g[i][None, :]
        w_proj = ple_w_proj[i].astype(BF16)
        w_gate = ple_w_gate[i].astype(BF16)
        if i % 2 == 0:
            w_in = attn_w_in[j][:, perm].astype(BF16)
            outs = _attn_in_proj(x, cos, sin, pre_g, w_in)
            og = _attention(outs[:9], outs[9])
            x = _attn_out_proj(og, x, p, i, post_g, attn_w_out[j].astype(BF16), w_proj, w_gate)
        else:
            x = _conv_layer(x, p, i, pre_g, conv_w_in[j].astype(BF16), conv_dw_w[j],
                            conv_dw_b[j][None, :], conv_ln_g[j][None, :], conv_ln_b[j][None, :],
                            conv_w_out[j].astype(BF16), post_g, w_proj, w_gate)
    return x
```

```python
import functools

import jax
import jax.numpy as jnp
import numpy as np
from jax import lax
from jax.experimental import pallas as pl
from jax.experimental.pallas import tpu as pltpu

D_MODEL = 1024
PLE_DIM = 256
HEAD_DIM = 64
N_HEADS = 16
N_PAIRS = N_HEADS // 2
ATTN_WIDTH = N_HEADS * HEAD_DIM
DILATIONS = (1, 4, 16)
N_BACK = 128
N_GROUPS = 3
ROPE_THETA = 10000.0
CONV_WIDTH = 31
CONV_HALO = 32
RMS_EPS = 1e-6
LN_EPS = 1e-5
NEG_INF = -1e30
LANES = 128
QK_SCALE = HEAD_DIM ** -0.5

F32 = jnp.float32
BF16 = jnp.bfloat16

VMEM_LIMIT = 56 * 1024 * 1024


def _const_spec(shape):
    return pl.BlockSpec(shape, lambda *_: (0,) * len(shape), pipeline_mode=pl.Buffered(1))


def _rmsnorm(x, g):
    return x * lax.rsqrt(jnp.mean(x * x, axis=-1, keepdims=True) + RMS_EPS) * g


def _sigmoid(x):
    return 1.0 / (1.0 + jnp.exp(-x))


def _rope_kernel(pos_ref, invf_ref, cos_ref, sin_ref):
    ang = pos_ref[0].astype(F32) * invf_ref[...]
    lane = lax.broadcasted_iota(jnp.int32, ang.shape, 1)
    cos_ref[0] = jnp.cos(ang)
    sin_ref[0] = jnp.where(lane < LANES // 2, -jnp.sin(ang), jnp.sin(ang))


def _rope_tables(positions):
    B, S = positions.shape
    ts = 512
    inv_freq = 1.0 / (ROPE_THETA ** (jnp.arange(0, HEAD_DIM, 2, dtype=F32) / HEAD_DIM))
    invf = jnp.tile(inv_freq, LANES // (HEAD_DIM // 2))[None, :]
    return pl.pallas_call(
        _rope_kernel,
        grid=(B, S // ts),
        in_specs=[pl.BlockSpec((1, ts, 1), lambda b, i: (b, i, 0)),
                  pl.BlockSpec((1, LANES), lambda b, i: (0, 0))],
        out_specs=[pl.BlockSpec((1, ts, LANES), lambda b, i: (b, i, 0))] * 2,
        out_shape=[jax.ShapeDtypeStruct((B, S, LANES), F32)] * 2,
        name="rope_tables",
    )(positions[:, :, None], invf)


def _deinterleave(ref, lead, d, rows):
    parts = [ref[lead + (pl.ds(r, rows // d, stride=d), slice(None))] for r in range(d)]
    return jnp.concatenate(parts, axis=0)


def _attn_in_kernel(x_ref, cos_ref, sin_ref, g_ref, w_ref, *refs, tm):
    out_refs = refs[:10]
    hs_ref = refs[10]
    qkv_refs = [out_refs[3 * g:3 * g + 3] for g in range(N_GROUPS)]
    z_ref = out_refs[9]

    hn = _rmsnorm(x_ref[0], g_ref[...])
    for c in range(N_PAIRS):
        hs_ref[c] = hn[:, c * LANES:(c + 1) * LANES]
    h0 = hn.astype(BF16)

    for g, d in enumerate(DILATIONS):
        if d == 1:
            hg, cg, sg = h0, cos_ref[0], sin_ref[0]
        else:
            hg = jnp.concatenate(
                [_deinterleave(hs_ref, (c,), d, tm) for c in range(N_PAIRS)], axis=1).astype(BF16)
            cg = _deinterleave(cos_ref, (0,), d, tm)
            sg = _deinterleave(sin_ref, (0,), d, tm)
        for part in range(3):
            col0 = (part * N_GROUPS + g) * ATTN_WIDTH
            for cc in range(ATTN_WIDTH // 256):
                acc = jnp.dot(hg, w_ref[:, col0 + cc * 256:col0 + (cc + 1) * 256],
                              preferred_element_type=F32)
                for half in range(2):
                    t = acc[:, half * LANES:(half + 1) * LANES]
                    if part < 2:
                        t = t * cg + pltpu.roll(t, LANES // 2, axis=1) * sg
                    if part == 0:
                        t = t * QK_SCALE
                    qkv_refs[g][part][0, cc * 2 + half] = (
                        t.reshape(d, tm // d, LANES).astype(BF16))
    zc0 = 3 * N_GROUPS * ATTN_WIDTH
    for cc in range(ATTN_WIDTH // 256):
        acc = jnp.dot(h0, w_ref[:, zc0 + cc * 256:zc0 + (cc + 1) * 256],
                      preferred_element_type=F32)
        for half in range(2):
            z_ref[0, cc * 2 + half] = acc[:, half * LANES:(half + 1) * LANES].astype(BF16)


def _attn_in_proj(x, cos, sin, g, w):
    B, S, _ = x.shape
    tm = 256
    ncols = w.shape[1]
    out_shape, out_specs = [], []
    for d in DILATIONS:
        L = S // d
        for _ in range(3):
            out_shape.append(jax.ShapeDtypeStruct((B, N_PAIRS, d, L, LANES), BF16))
            out_specs.append(pl.BlockSpec((1, N_PAIRS, d, tm // d, LANES),
                                          lambda b, i: (b, 0, 0, i, 0)))
    out_shape.append(jax.ShapeDtypeStruct((B, N_PAIRS, S, LANES), BF16))
    out_specs.append(pl.BlockSpec((1, N_PAIRS, tm, LANES), lambda b, i: (b, 0, i, 0)))
    return pl.pallas_call(
        functools.partial(_attn_in_kernel, tm=tm),
        grid=(B, S // tm),
        in_specs=[pl.BlockSpec((1, tm, D_MODEL), lambda b, i: (b, i, 0)),
                  pl.BlockSpec((1, tm, LANES), lambda b, i: (b, i, 0)),
                  pl.BlockSpec((1, tm, LANES), lambda b, i: (b, i, 0)),
                  _const_spec((1, D_MODEL)),
                  _const_spec((D_MODEL, ncols))],
        out_specs=out_specs,
        out_shape=out_shape,
        scratch_shapes=[pltpu.VMEM((N_PAIRS, tm, LANES), F32)],
        compiler_params=pltpu.CompilerParams(
            dimension_semantics=("parallel", "parallel"), vmem_limit_bytes=VMEM_LIMIT),
        name="attn_in_proj",
    )(x, cos, sin, g, w)


def _attn_kernel(*refs, S, unroll):
    qkv_refs = [refs[3 * g:3 * g + 3] for g in range(N_GROUPS)]
    z_ref, bias_ref, o_ref, acc_scr, m_scr, l_scr = refs[9:15]
    T = N_BACK

    lane = lax.broadcasted_iota(jnp.int32, (T, LANES), 1)
    q_is_h0 = (lane // 32) % 2 == 0
    v_is_h0 = lane < HEAD_DIM
    ones = jnp.ones((2 * T, LANES), BF16)

    def one_block(g, d, nb, idx):
        q_ref, k_ref, v_ref = qkv_refs[g]
        r = idx // nb
        n = idx % nb
        start = pl.multiple_of(n * T, T)
        kstart = pl.multiple_of(jnp.maximum(start - T, 0), T)
        q = q_ref[0, 0, r, pl.ds(start, T), :]
        k = k_ref[0, 0, r, pl.ds(kstart, 2 * T), :]
        v = v_ref[0, 0, r, pl.ds(kstart, 2 * T), :]
        zero = jnp.zeros_like(q)
        qq = jnp.concatenate([jnp.where(q_is_h0, q, zero), jnp.where(q_is_h0, zero, q)], axis=0)
        s = lax.dot_general(qq, k, (((1,), (1,)), ((), ())), preferred_element_type=F32)
        s = s + bias_ref[jnp.where(n == 0, 1, 0)]
        m = jnp.max(s, axis=-1, keepdims=True)
        p = jnp.exp(s - m).astype(BF16)
        res = jnp.dot(p, jnp.concatenate([v, ones], axis=1), preferred_element_type=F32)
        acc = jnp.where(v_is_h0, res[:T, :LANES], res[T:, :LANES])
        lb = jnp.where(v_is_h0, res[:T, LANES:], res[T:, LANES:])
        mb = jnp.where(v_is_h0, m[:T], m[T:])
        if d == 1:
            rows = pl.ds(start, T)
        else:
            rows = pl.ds(start * d + r, T, stride=d)
        acc_scr[g, rows, :] = acc
        m_scr[g, rows, :] = mb
        l_scr[g, rows, :] = lb

    for g, d in enumerate(DILATIONS):
        nb = (S // d) // T

        def blocks(i, carry, g=g, d=d, nb=nb):
            for u in range(unroll):
                one_block(g, d, nb, i * unroll + u)
            return carry

        lax.fori_loop(0, d * nb // unroll, blocks, 0)

    def combine(c, carry):
        rows = pl.ds(pl.multiple_of(c * T, T), T)
        ms = [m_scr[g, rows, :] for g in range(N_GROUPS)]
        mx = jnp.maximum(jnp.maximum(ms[0], ms[1]), ms[2])
        num = jnp.zeros((T, LANES), F32)
        den = jnp.zeros((T, LANES), F32)
        for g in range(N_GROUPS):
            w = jnp.exp(ms[g] - mx)
            num = num + w * acc_scr[g, rows, :]
            den = den + w * l_scr[g, rows, :]
        zf = z_ref[0, 0, rows, :].astype(F32)
        o_ref[0, 0, rows, :] = ((num / den) * (zf * _sigmoid(zf))).astype(BF16)
        return carry

    lax.fori_loop(0, S // T, combine, 0)


def _band_bias():
    T = N_BACK
    row = np.arange(T)[:, None]
    col = np.arange(2 * T)[None, :]
    tables = []
    for off in (T, 0):
        dist = off + row - col
        b = np.where((dist >= 0) & (dist <= T), 0.0, NEG_INF).astype(np.float32)
        tables.append(np.concatenate([b, b], axis=0))
    return jnp.asarray(np.stack(tables))


def _attention(qkv, z):
    B, _, S, _ = z.shape
    in_specs = []
    for d in DILATIONS:
        L = S // d
        in_specs += [pl.BlockSpec((1, 1, d, L, LANES), lambda b, p: (b, p, 0, 0, 0))] * 3
    in_specs.append(pl.BlockSpec((1, 1, S, LANES), lambda b, p: (b, p, 0, 0)))
    in_specs.append(_const_spec((2, 2 * N_BACK, 2 * N_BACK)))
    return pl.pallas_call(
        functools.partial(_attn_kernel, S=S, unroll=8),
        grid=(B, N_PAIRS),
        in_specs=in_specs,
        out_specs=pl.BlockSpec((1, 1, S, LANES), lambda b, p: (b, p, 0, 0)),
        out_shape=jax.ShapeDtypeStruct((B, N_PAIRS, S, LANES), BF16),
        scratch_shapes=[pltpu.VMEM((N_GROUPS, S, LANES), F32)] * 3,
        compiler_params=pltpu.CompilerParams(
            dimension_semantics=("parallel", "parallel"), vmem_limit_bytes=VMEM_LIMIT),
        name="dilated_attention",
    )(*qkv, z, _band_bias())


def _layer_tail(y, x, p_bf16, post_g, w_proj_ref, w_gate_ref):
    x1 = x + _rmsnorm(y, post_g)
    gate = _sigmoid(jnp.dot(x1.astype(BF16), w_gate_ref[...], preferred_element_type=F32))
    pe = jnp.dot(p_bf16, w_proj_ref[...], preferred_element_type=F32)
    return x1 + pe * gate


def _attn_out_kernel(o_ref, x_ref, p_ref, g_ref, wo_ref, wp_ref, wg_ref, out_ref):
    o = jnp.concatenate([o_ref[0, c] for c in range(N_PAIRS)], axis=1)
    y = jnp.dot(o, wo_ref[...], preferred_element_type=F32)
    out_ref[0] = _layer_tail(y, x_ref[0], p_ref[0, 0].astype(BF16), g_ref[...], wp_ref, wg_ref)


def _attn_out_proj(og, x, p, layer, post_g, w_out, w_proj, w_gate):
    B, S, _ = x.shape
    tm = 512
    return pl.pallas_call(
        _attn_out_kernel,
        grid=(B, S // tm),
        in_specs=[pl.BlockSpec((1, N_PAIRS, tm, LANES), lambda b, i: (b, 0, i, 0)),
                  pl.BlockSpec((1, tm, D_MODEL), lambda b, i: (b, i, 0)),
                  pl.BlockSpec((1, 1, tm, PLE_DIM), lambda b, i: (layer, b, i, 0)),
                  _const_spec((1, D_MODEL)),
                  _const_spec((ATTN_WIDTH, D_MODEL)),
                  _const_spec((PLE_DIM, D_MODEL)),
                  _const_spec((D_MODEL, D_MODEL))],
        out_specs=pl.BlockSpec((1, tm, D_MODEL), lambda b, i: (b, i, 0)),
        out_shape=jax.ShapeDtypeStruct((B, S, D_MODEL), F32),
        compiler_params=pltpu.CompilerParams(
            dimension_semantics=("parallel", "parallel"), vmem_limit_bytes=VMEM_LIMIT),
        name="attn_out_proj",
    )(og, x, p, post_g, w_out, w_proj, w_gate)


def _conv_kernel(xh_ref, x_ref, p_ref, pre_g_ref, w_in_ref, dw_w_ref, dw_b_ref, ln_g_ref,
                 ln_b_ref, w_out_ref, post_g_ref, wp_ref, wg_ref, out_ref, u_scr, cv_scr, *, tm):
    C = D_MODEL
    x = x_ref[0]
    xa = jnp.concatenate([xh_ref[0], x], axis=0)
    h = _rmsnorm(xa, pre_g_ref[...]).astype(BF16)
    ab = jnp.dot(h, w_in_ref[:, :2 * C], preferred_element_type=F32)
    u = ab[:, :C] * _sigmoid(ab[:, C:])
    row = lax.broadcasted_iota(jnp.int32, u.shape, 0)
    u = jnp.where((row >= CONV_HALO) | (pl.program_id(1) > 0), u, 0.0)
    for c in range(C // LANES):
        u_scr[c] = u[:, c * LANES:(c + 1) * LANES]
    z = jnp.dot(h[CONV_HALO:], w_in_ref[:, 2 * C:], preferred_element_type=F32)

    RB = 128
    first = CONV_HALO - (CONV_WIDTH - 1)

    def conv_rows(i, carry):
        r0 = pl.multiple_of(i * RB, RB)
        for c in range(C // LANES):
            lanes = slice(c * LANES, (c + 1) * LANES)
            win = u_scr.at[c, pl.ds(r0, RB + CONV_HALO), :]
            acc = jnp.broadcast_to(dw_b_ref[:, lanes], (RB, LANES))
            for w in range(CONV_WIDTH):
                acc = acc + win[pl.ds(first + w, RB), :] * dw_w_ref[pl.ds(w, 1), lanes]
            cv_scr[c, pl.ds(r0, RB), :] = acc
        return carry

    lax.fori_loop(0, tm // RB, conv_rows, 0)

    cv = jnp.concatenate([cv_scr[c] for c in range(C // LANES)], axis=1)
    mu = jnp.mean(cv, axis=-1, keepdims=True)
    cen = cv - mu
    var = jnp.mean(cen * cen, axis=-1, keepdims=True)
    y = cen * lax.rsqrt(var + LN_EPS) * ln_g_ref[...] + ln_b_ref[...]
    y = y * _sigmoid(y)
    yg = (y * (z * _sigmoid(z))).astype(BF16)
    out = jnp.dot(yg, w_out_ref[...], preferred_element_type=F32)
    out_ref[0] = _layer_tail(out, x, p_ref[0, 0].astype(BF16), post_g_ref[...], wp_ref, wg_ref)


def _conv_layer(x, p, layer, pre_g, w_in, dw_w, dw_b, ln_g, ln_b, w_out, post_g, w_proj, w_gate):
    B, S, C = x.shape
    tm = 512
    hb = tm // CONV_HALO
    return pl.pallas_call(
        functools.partial(_conv_kernel, tm=tm),
        grid=(B, S // tm),
        in_specs=[pl.BlockSpec((1, CONV_HALO, C), lambda b, i: (b, jnp.maximum(i * hb - 1, 0), 0)),
                  pl.BlockSpec((1, tm, C), lambda b, i: (b, i, 0)),
                  pl.BlockSpec((1, 1, tm, PLE_DIM), lambda b, i: (layer, b, i, 0)),
                  _const_spec((1, C)),
                  _const_spec((C, 3 * C)),
                  _const_spec((CONV_WIDTH, C)),
                  _const_spec((1, C)),
                  _const_spec((1, C)),
                  _const_spec((1, C)),
                  _const_spec((C, C)),
                  _const_spec((1, C)),
                  _const_spec((PLE_DIM, C)),
                  _const_spec((C, C))],
        out_specs=pl.BlockSpec((1, tm, C), lambda b, i: (b, i, 0)),
        out_shape=jax.ShapeDtypeStruct((B, S, C), F32),
        scratch_shapes=[pltpu.VMEM((C // LANES, tm + CONV_HALO, LANES), F32),
                        pltpu.VMEM((C // LANES, tm, LANES), F32)],
        compiler_params=pltpu.CompilerParams(
            dimension_semantics=("parallel", "parallel"), vmem_limit_bytes=VMEM_LIMIT),
        name="conv_layer",
    )(x, x, p, pre_g, w_in, dw_w, dw_b, ln_g, ln_b, w_out, post_g, w_proj, w_gate)


def _qk_column_permutation():
    cols = np.arange(3 * N_GROUPS * ATTN_WIDTH + ATTN_WIDTH)
    lane = np.arange(LANES)
    within = ((lane // 32) % 2) * HEAD_DIM + (lane // 64) * 32 + lane % 32
    for blk in range(2 * N_GROUPS * N_PAIRS):
        cols[blk * LANES:(blk + 1) * LANES] = blk * LANES + within
    return cols


def kernel(x, p, positions, pre_norm_g, post_norm_g, attn_w_in, attn_w_out, conv_w_in, conv_dw_w,
           conv_dw_b, conv_ln_g, conv_ln_b, conv_w_out, ple_w_proj, ple_w_gate):
    depth = p.shape[0]
    cos, sin = _rope_tables(positions)
    perm = _qk_column_permutation()
    for i in range(depth):
        j = i // 2
        pre_g = pre_norm_g[i][None, :]
        post_g = post_norm_g[i][None, :]
        w_proj = ple_w_proj[i].astype(BF16)
        w_gate = ple_w_gate[i].astype(BF16)
        if i % 2 == 0:
            w_in = attn_w_in[j][:, perm].astype(BF16)
            outs = _attn_in_proj(x, cos, sin, pre_g, w_in)
            og = _attention(outs[:9], outs[9])
            x = _attn_out_proj(og, x, p, i, post_g, attn_w_out[j].astype(BF16), w_proj, w_gate)
        else:
            x = _conv_layer(x, p, i, pre_g, conv_w_in[j].astype(BF16), conv_dw_w[j],
                            conv_dw_b[j][None, :], conv_ln_g[j][None, :], conv_ln_b[j][None, :],
                            conv_w_out[j].astype(BF16), post_g, w_proj, w_gate)
    return x
```

```python
import functools

import jax
import jax.numpy as jnp
import numpy as np
from jax import lax
from jax.experimental import pallas as pl
from jax.experimental.pallas import tpu as pltpu

D_MODEL = 1024
PLE_DIM = 256
HEAD_DIM = 64
N_HEADS = 16
N_PAIRS = N_HEADS // 2
ATTN_WIDTH = N_HEADS * HEAD_DIM
DILATIONS = (1, 4, 16)
N_BACK = 128
N_GROUPS = 3
ROPE_THETA = 10000.0
CONV_WIDTH = 31
CONV_HALO = 32
RMS_EPS = 1e-6
LN_EPS = 1e-5
NEG_INF = -1e30
LANES = 128
QK_SCALE = HEAD_DIM ** -0.5
PITCH16 = 24

F32 = jnp.float32
BF16 = jnp.bfloat16

VMEM_LIMIT = 56 * 1024 * 1024


def _const_spec(shape):
    return pl.BlockSpec(shape, lambda *_: (0,) * len(shape), pipeline_mode=pl.Buffered(1))


def _rmsnorm(x, g):
    return x * lax.rsqrt(jnp.mean(x * x, axis=-1, keepdims=True) + RMS_EPS) * g


def _sigmoid(x):
    return 1.0 / (1.0 + jnp.exp(-x))


def _rope_kernel(pos_ref, invf_ref, cos_ref, sin_ref):
    ang = pos_ref[0].astype(F32) * invf_ref[...]
    lane = lax.broadcasted_iota(jnp.int32, ang.shape, 1)
    cos_ref[0] = jnp.cos(ang)
    sin_ref[0] = jnp.where(lane < LANES // 2, -jnp.sin(ang), jnp.sin(ang))


def _rope_tables(positions):
    B, S = positions.shape
    ts = 512
    inv_freq = 1.0 / (ROPE_THETA ** (jnp.arange(0, HEAD_DIM, 2, dtype=F32) / HEAD_DIM))
    invf = jnp.tile(inv_freq, LANES // (HEAD_DIM // 2))[None, :]
    return pl.pallas_call(
        _rope_kernel,
        grid=(B, S // ts),
        in_specs=[pl.BlockSpec((1, ts, 1), lambda b, i: (b, i, 0)),
                  pl.BlockSpec((1, LANES), lambda b, i: (0, 0))],
        out_specs=[pl.BlockSpec((1, ts, LANES), lambda b, i: (b, i, 0))] * 2,
        out_shape=[jax.ShapeDtypeStruct((B, S, LANES), F32)] * 2,
        name="rope_tables",
    )(positions[:, :, None], invf)


def _deinterleave(ref, lead, d, rows):
    parts = [ref[lead + (pl.ds(r, rows // d, stride=d), slice(None))] for r in range(d)]
    return jnp.concatenate(parts, axis=0)


def _attn_in_kernel(x_ref, cos_ref, sin_ref, g_ref, w_ref, *refs, tm):
    out_refs = refs[:10]
    hs_ref = refs[10]
    qkv_refs = [out_refs[3 * g:3 * g + 3] for g in range(N_GROUPS)]
    zg_ref = out_refs[9]

    hn = _rmsnorm(x_ref[0], g_ref[...])
    for c in range(N_PAIRS):
        hs_ref[c] = hn[:, c * LANES:(c + 1) * LANES]
    h0 = hn.astype(BF16)

    for g, d in enumerate(DILATIONS):
        if d == 1:
            hg, cg, sg = h0, cos_ref[0], sin_ref[0]
        else:
            hg = jnp.concatenate(
                [_deinterleave(hs_ref, (c,), d, tm) for c in range(N_PAIRS)], axis=1).astype(BF16)
            cg = _deinterleave(cos_ref, (0,), d, tm)
            sg = _deinterleave(sin_ref, (0,), d, tm)
        for part in range(3):
            col0 = (part * N_GROUPS + g) * ATTN_WIDTH
            for cc in range(ATTN_WIDTH // 256):
                acc = jnp.dot(hg, w_ref[:, col0 + cc * 256:col0 + (cc + 1) * 256],
                              preferred_element_type=F32)
                for half in range(2):
                    t = acc[:, half * LANES:(half + 1) * LANES]
                    if part < 2:
                        t = t * cg + pltpu.roll(t, LANES // 2, axis=1) * sg
                    if part == 0:
                        t = t * QK_SCALE
                    qkv_refs[g][part][0, cc * 2 + half] = (
                        t.reshape(d, tm // d, LANES).astype(BF16))
    zc0 = 3 * N_GROUPS * ATTN_WIDTH
    for cc in range(ATTN_WIDTH // 256):
        z = jnp.dot(h0, w_ref[:, zc0 + cc * 256:zc0 + (cc + 1) * 256],
                    preferred_element_type=F32)
        zg = z * _sigmoid(z)
        for half in range(2):
            zg_ref[0, cc * 2 + half] = zg[:, half * LANES:(half + 1) * LANES].astype(BF16)


def _attn_in_proj(x, cos, sin, g, w):
    B, S, _ = x.shape
    tm = 256
    ncols = w.shape[1]
    out_shape, out_specs = [], []
    for d in DILATIONS:
        L = S // d
        for _ in range(3):
            out_shape.append(jax.ShapeDtypeStruct((B, N_PAIRS, d, L, LANES), BF16))
            out_specs.append(pl.BlockSpec((1, N_PAIRS, d, tm // d, LANES),
                                          lambda b, i: (b, 0, 0, i, 0)))
    out_shape.append(jax.ShapeDtypeStruct((B, N_PAIRS, S, LANES), BF16))
    out_specs.append(pl.BlockSpec((1, N_PAIRS, tm, LANES), lambda b, i: (b, 0, i, 0)))
    return pl.pallas_call(
        functools.partial(_attn_in_kernel, tm=tm),
        grid=(B, S // tm),
        in_specs=[pl.BlockSpec((1, tm, D_MODEL), lambda b, i: (b, i, 0)),
                  pl.BlockSpec((1, tm, LANES), lambda b, i: (b, i, 0)),
                  pl.BlockSpec((1, tm, LANES), lambda b, i: (b, i, 0)),
                  _const_spec((1, D_MODEL)),
                  _const_spec((D_MODEL, ncols))],
        out_specs=out_specs,
        out_shape=out_shape,
        scratch_shapes=[pltpu.VMEM((N_PAIRS, tm, LANES), F32)],
        compiler_params=pltpu.CompilerParams(
            dimension_semantics=("parallel", "parallel"), vmem_limit_bytes=VMEM_LIMIT),
        name="attn_in_proj",
    )(x, cos, sin, g, w)


def _attn_kernel(*refs, S, unroll):
    qkv_refs = [refs[3 * g:3 * g + 3] for g in range(N_GROUPS)]
    zg_ref, bias_ref, o_ref = refs[9:12]
    out01_scr, lse01_scr, out2_scr, lse2_scr, p_scr, m_scr = refs[12:18]
    T = N_BACK
    n_steps = (S // T) // unroll

    lane = lax.broadcasted_iota(jnp.int32, (T, LANES), 1)
    q_is_h0 = (lane // 32) % 2 == 0
    v_is_h0 = lane < HEAD_DIM
    ones = jnp.ones((2 * T, LANES), BF16)

    def block_coords(g, idx):
        nb = (S // DILATIONS[g]) // T
        r = idx // nb
        n = idx % nb
        start = pl.multiple_of(n * T, T)
        kstart = pl.multiple_of(jnp.maximum(start - T, 0), T)
        return r, n, start, kstart

    def stage_a(g, step, slot):
        q_ref, k_ref, _ = qkv_refs[g]
        for u in range(unroll):
            r, n, start, kstart = block_coords(g, step * unroll + u)
            q = q_ref[0, 0, r, pl.ds(start, T), :]
            k = k_ref[0, 0, r, pl.ds(kstart, 2 * T), :]
            zero = jnp.zeros_like(q)
            qq = jnp.concatenate(
                [jnp.where(q_is_h0, q, zero), jnp.where(q_is_h0, zero, q)], axis=0)
            s = lax.dot_general(qq, k, (((1,), (1,)), ((), ())), preferred_element_type=F32)
            s = s + bias_ref[jnp.where(n == 0, 1, 0)]
            m = jnp.max(s, axis=-1, keepdims=True)
            p_scr[slot, u] = jnp.exp(s - m).astype(BF16)
            m_scr[slot, u] = jnp.where(v_is_h0, m[:T], m[T:])

    def stage_b(g, step, slot):
        d = DILATIONS[g]
        v_ref = qkv_refs[g][2]
        for u in range(unroll):
            r, n, start, kstart = block_coords(g, step * unroll + u)
            v = v_ref[0, 0, r, pl.ds(kstart, 2 * T), :]
            res = jnp.dot(p_scr[slot, u], jnp.concatenate([v, ones], axis=1),
                          preferred_element_type=F32)
            acc = jnp.where(v_is_h0, res[:T, :LANES], res[T:, :LANES])
            l = jnp.where(v_is_h0, res[:T, LANES:], res[T:, LANES:])
            out = acc * (1.0 / l)
            lse = m_scr[slot, u] + jnp.log(l)
            if d == 1:
                out01_scr[0, pl.ds(start, T), :] = out
                lse01_scr[0, pl.ds(start, T), :] = lse
            elif d == 4:
                rows = pl.ds(start * d + r, T, stride=d)
                out01_scr[1, rows, :] = out
                lse01_scr[1, rows, :] = lse
            else:
                rows = pl.ds(start * PITCH16 + r, T, stride=PITCH16)
                out2_scr[rows, :] = out
                lse2_scr[rows, :] = lse

    stage_a(0, 0, 0)
    for g in range(N_GROUPS):
        base = g * n_steps

        def steady(i, carry, g=g, base=base):
            slot = (base + i) % 2
            stage_b(g, i, slot)
            stage_a(g, i + 1, 1 - slot)
            return carry

        lax.fori_loop(0, n_steps - 1, steady, 0)
        last_slot = (base + n_steps - 1) % 2
        stage_b(g, n_steps - 1, last_slot)
        if g + 1 < N_GROUPS:
            stage_a(g + 1, 0, 1 - last_slot)

    def combine(c, carry):
        rows = pl.ds(pl.multiple_of(c * T, T), T)
        base2 = pl.multiple_of(c * (T // 16) * PITCH16, 8)
        win2 = [pl.ds(base2 + j * PITCH16, 16) for j in range(T // 16)]
        lses = [lse01_scr[0, rows, :], lse01_scr[1, rows, :],
                jnp.concatenate([lse2_scr[w, :] for w in win2], axis=0)]
        outs = [out01_scr[0, rows, :], out01_scr[1, rows, :],
                jnp.concatenate([out2_scr[w, :] for w in win2], axis=0)]
        mx = jnp.maximum(jnp.maximum(lses[0], lses[1]), lses[2])
        num = jnp.zeros((T, LANES), F32)
        den = jnp.zeros((T, LANES), F32)
        for g in range(N_GROUPS):
            w = jnp.exp(lses[g] - mx)
            num = num + w * outs[g]
            den = den + w
        o_ref[0, 0, rows, :] = (num * (1.0 / den) * zg_ref[0, 0, rows, :].astype(F32)).astype(BF16)
        return carry

    lax.fori_loop(0, S // T, combine, 0)


def _band_bias():
    T = N_BACK
    row = np.arange(T)[:, None]
    col = np.arange(2 * T)[None, :]
    tables = []
    for off in (T, 0):
        dist = off + row - col
        b = np.where((dist >= 0) & (dist <= T), 0.0, NEG_INF).astype(np.float32)
        tables.append(np.concatenate([b, b], axis=0))
    return jnp.asarray(np.stack(tables))


def _attention(qkv, zg):
    B, _, S, _ = zg.shape
    T = N_BACK
    unroll = 16
    in_specs = []
    for d in DILATIONS:
        L = S // d
        in_specs += [pl.BlockSpec((1, 1, d, L, LANES), lambda b, p: (b, p, 0, 0, 0))] * 3
    in_specs.append(pl.BlockSpec((1, 1, S, LANES), lambda b, p: (b, p, 0, 0)))
    in_specs.append(_const_spec((2, 2 * T, 2 * T)))
    rows16 = S // 16 * PITCH16
    return pl.pallas_call(
        functools.partial(_attn_kernel, S=S, unroll=unroll),
        grid=(B, N_PAIRS),
        in_specs=in_specs,
        out_specs=pl.BlockSpec((1, 1, S, LANES), lambda b, p: (b, p, 0, 0)),
        out_shape=jax.ShapeDtypeStruct((B, N_PAIRS, S, LANES), BF16),
        scratch_shapes=[pltpu.VMEM((2, S, LANES), F32),
                        pltpu.VMEM((2, S, LANES), F32),
                        pltpu.VMEM((rows16, LANES), F32),
                        pltpu.VMEM((rows16, LANES), F32),
                        pltpu.VMEM((2, unroll, 2 * T, 2 * T), BF16),
                        pltpu.VMEM((2, unroll, T, LANES), F32)],
        compiler_params=pltpu.CompilerParams(
            dimension_semantics=("parallel", "parallel"), vmem_limit_bytes=VMEM_LIMIT),
        name="dilated_attention",
    )(*qkv, zg, _band_bias())


def _layer_tail(y, x, p_bf16, post_g, w_proj_ref, w_gate_ref):
    x1 = x + _rmsnorm(y, post_g)
    gate = _sigmoid(jnp.dot(x1.astype(BF16), w_gate_ref[...], preferred_element_type=F32))
    pe = jnp.dot(p_bf16, w_proj_ref[...], preferred_element_type=F32)
    return x1 + pe * gate


def _attn_out_kernel(o_ref, x_ref, p_ref, g_ref, wo_ref, wp_ref, wg_ref, out_ref):
    o = jnp.concatenate([o_ref[0, c] for c in range(N_PAIRS)], axis=1)
    y = jnp.dot(o, wo_ref[...], preferred_element_type=F32)
    out_ref[0] = _layer_tail(y, x_ref[0], p_ref[0, 0].astype(BF16), g_ref[...], wp_ref, wg_ref)


def _attn_out_proj(og, x, p, layer, post_g, w_out, w_proj, w_gate):
    B, S, _ = x.shape
    tm = 512
    return pl.pallas_call(
        _attn_out_kernel,
        grid=(B, S // tm),
        in_specs=[pl.BlockSpec((1, N_PAIRS, tm, LANES), lambda b, i: (b, 0, i, 0)),
                  pl.BlockSpec((1, tm, D_MODEL), lambda b, i: (b, i, 0)),
                  pl.BlockSpec((1, 1, tm, PLE_DIM), lambda b, i: (layer, b, i, 0)),
                  _const_spec((1, D_MODEL)),
                  _const_spec((ATTN_WIDTH, D_MODEL)),
                  _const_spec((PLE_DIM, D_MODEL)),
                  _const_spec((D_MODEL, D_MODEL))],
        out_specs=pl.BlockSpec((1, tm, D_MODEL), lambda b, i: (b, i, 0)),
        out_shape=jax.ShapeDtypeStruct((B, S, D_MODEL), F32),
        compiler_params=pltpu.CompilerParams(
            dimension_semantics=("parallel", "parallel"), vmem_limit_bytes=VMEM_LIMIT),
        name="attn_out_proj",
    )(og, x, p, post_g, w_out, w_proj, w_gate)


CONV_ROWS = 128
PROJ_COLS = 256


def _conv_kernel(xh_ref, xc_ref, xp_ref, p_ref, pre_g_ref, w_in_ref, dw_w_ref, dw_b_ref, ln_g_ref,
                 ln_b_ref, w_out_ref, post_g_ref, wp_ref, wg_ref, out_ref,
                 h_scr, u_scr, zg_scr, cv_scr, *, tm, tiles_per_seq):
    C = D_MODEL
    n_slabs = C // LANES
    per = PROJ_COLS // LANES
    t = pl.program_id(0)
    rd, wr = (t + 1) % 2, t % 2

    @pl.when(t == 0)
    def _():
        u_scr[1] = jnp.zeros(u_scr.shape[1:], F32)
        zg_scr[1] = jnp.zeros(zg_scr.shape[1:], BF16)

    xa = jnp.concatenate([xh_ref[...], xc_ref[...]], axis=0)
    h_scr[...] = _rmsnorm(xa, pre_g_ref[...]).astype(BF16)
    row = lax.broadcasted_iota(jnp.int32, (tm + CONV_HALO, PROJ_COLS), 0)
    keep = (row >= CONV_HALO) | (t % tiles_per_seq != 0)
    first = CONV_HALO - (CONV_WIDTH - 1)
    n_col_blocks = C // PROJ_COLS

    def step(i, carry):
        r0 = pl.multiple_of(i * CONV_ROWS, CONV_ROWS)
        for c in range(n_slabs):
            lanes = slice(c * LANES, (c + 1) * LANES)
            win = u_scr.at[rd, c, pl.ds(r0, CONV_ROWS + CONV_HALO), :]
            acc = jnp.broadcast_to(dw_b_ref[:, lanes], (CONV_ROWS, LANES))
            for w in range(CONV_WIDTH):
                acc = acc + win[pl.ds(first + w, CONV_ROWS), :] * dw_w_ref[pl.ds(w, 1), lanes]
            cv_scr[c, pl.ds(r0, CONV_ROWS), :] = acc
        h = h_scr[...]
        a = jnp.dot(h, w_in_ref[i], preferred_element_type=F32)
        b = jnp.dot(h, w_in_ref[n_col_blocks + i], preferred_element_type=F32)
        u = jnp.where(keep, a * _sigmoid(b), 0.0)
        z = jnp.dot(h[CONV_HALO:], w_in_ref[2 * n_col_blocks + i], preferred_element_type=F32)
        zg = (z * _sigmoid(z)).astype(BF16)
        for k in range(per):
            u_scr[wr, i * per + k] = u[:, k * LANES:(k + 1) * LANES]
            zg_scr[wr, i * per + k] = zg[:, k * LANES:(k + 1) * LANES]
        return carry

    assert tm // CONV_ROWS == n_col_blocks
    lax.fori_loop(0, n_col_blocks, step, 0)

    cv = jnp.concatenate([cv_scr[c] for c in range(n_slabs)], axis=1)
    mu = jnp.mean(cv, axis=-1, keepdims=True)
    cen = cv - mu
    var = jnp.mean(cen * cen, axis=-1, keepdims=True)
    y = cen * lax.rsqrt(var + LN_EPS) * ln_g_ref[...] + ln_b_ref[...]
    y = y * _sigmoid(y)
    zg = jnp.concatenate([zg_scr[rd, c] for c in range(n_slabs)], axis=1)
    yg = (y * zg.astype(F32)).astype(BF16)
    out = jnp.dot(yg, w_out_ref[...], preferred_element_type=F32)
    out_ref[...] = _layer_tail(out, xp_ref[...], p_ref[0].astype(BF16), post_g_ref[...],
                               wp_ref, wg_ref)


def _conv_layer(x, p, layer, pre_g, w_in, dw_w, dw_b, ln_g, ln_b, w_out, post_g, w_proj, w_gate):
    B, S, C = x.shape
    tm = 512
    n_tiles = B * S // tm
    hb = tm // CONV_HALO
    x2 = x.reshape(B * S, C)
    p2 = p.reshape(p.shape[0], B * S, PLE_DIM)
    w_blocks = w_in.reshape(C, 3 * C // PROJ_COLS, PROJ_COLS).transpose(1, 0, 2)
    cur = lambda t: jnp.minimum(t, n_tiles - 1)
    prev = lambda t: jnp.maximum(t - 1, 0)
    out = pl.pallas_call(
        functools.partial(_conv_kernel, tm=tm, tiles_per_seq=S // tm),
        grid=(n_tiles + 1,),
        in_specs=[pl.BlockSpec((CONV_HALO, C), lambda t: (jnp.maximum(cur(t) * hb - 1, 0), 0)),
                  pl.BlockSpec((tm, C), lambda t: (cur(t), 0)),
                  pl.BlockSpec((tm, C), lambda t: (prev(t), 0)),
                  pl.BlockSpec((1, tm, PLE_DIM), lambda t: (layer, prev(t), 0)),
                  _const_spec((1, C)),
                  _const_spec(w_blocks.shape),
                  _const_spec((CONV_WIDTH, C)),
                  _const_spec((1, C)),
                  _const_spec((1, C)),
                  _const_spec((1, C)),
                  _const_spec((C, C)),
                  _const_spec((1, C)),
                  _const_spec((PLE_DIM, C)),
                  _const_spec((C, C))],
        out_specs=pl.BlockSpec((tm, C), lambda t: (prev(t), 0)),
        out_shape=jax.ShapeDtypeStruct((B * S, C), F32),
        scratch_shapes=[pltpu.VMEM((tm + CONV_HALO, C), BF16),
                        pltpu.VMEM((2, C // LANES, tm + CONV_HALO, LANES), F32),
                        pltpu.VMEM((2, C // LANES, tm, LANES), BF16),
                        pltpu.VMEM((C // LANES, tm, LANES), F32)],
        compiler_params=pltpu.CompilerParams(
            dimension_semantics=("arbitrary",), vmem_limit_bytes=VMEM_LIMIT),
        name="conv_layer",
    )(x2, x2, x2, p2, pre_g, w_blocks, dw_w, dw_b, ln_g, ln_b, w_out, post_g, w_proj, w_gate)
    return out.reshape(B, S, C)


def _permute_qk_columns(w):
    n_qk = 2 * N_GROUPS * ATTN_WIDTH
    qk = w[:, :n_qk].reshape(w.shape[0], n_qk // LANES, 2, 2, HEAD_DIM // 2)
    qk = qk.transpose(0, 1, 3, 2, 4).reshape(w.shape[0], n_qk)
    return jnp.concatenate([qk, w[:, n_qk:]], axis=1)


def kernel(x, p, positions, pre_norm_g, post_norm_g, attn_w_in, attn_w_out, conv_w_in, conv_dw_w,
           conv_dw_b, conv_ln_g, conv_ln_b, conv_w_out, ple_w_proj, ple_w_gate):
    depth = p.shape[0]
    cos, sin = _rope_tables(positions)
    for i in range(depth):
        j = i // 2
        pre_g = pre_norm_g[i][None, :]
        post_g = post_norm_g[i][None, :]
        w_proj = ple_w_proj[i].astype(BF16)
        w_gate = ple_w_gate[i].astype(BF16)
        if i % 2 == 0:
            w_in = _permute_qk_columns(attn_w_in[j].astype(BF16))
            outs = _attn_in_proj(x, cos, sin, pre_g, w_in)
            og = _attention(outs[:9], outs[9])
            x = _attn_out_proj(og, x, p, i, post_g, attn_w_out[j].astype(BF16), w_proj, w_gate)
        else:
            x = _conv_layer(x, p, i, pre_g, conv_w_in[j].astype(BF16), conv_dw_w[j],
                            conv_dw_b[j][None, :], conv_ln_g[j][None, :], conv_ln_b[j][None, :],
                            conv_w_out[j].astype(BF16), post_g, w_proj, w_gate)
    return x
```

```python
import functools

import jax
import jax.numpy as jnp
import numpy as np
from jax import lax
from jax.experimental import pallas as pl
from jax.experimental.pallas import tpu as pltpu

D_MODEL = 1024
PLE_DIM = 256
HEAD_DIM = 64
N_HEADS = 16
N_PAIRS = N_HEADS // 2
ATTN_WIDTH = N_HEADS * HEAD_DIM
DILATIONS = (1, 4, 16)
N_BACK = 128
N_GROUPS = 3
ROPE_THETA = 10000.0
CONV_WIDTH = 31
CONV_HALO = 32
RMS_EPS = 1e-6
LN_EPS = 1e-5
NEG_INF = -1e30
LANES = 128
QK_SCALE = HEAD_DIM ** -0.5
PITCH16 = 24

F32 = jnp.float32
BF16 = jnp.bfloat16

VMEM_LIMIT = 56 * 1024 * 1024


def _const_spec(shape):
    return pl.BlockSpec(shape, lambda *_: (0,) * len(shape), pipeline_mode=pl.Buffered(1))


def _rmsnorm(x, g):
    return x * lax.rsqrt(jnp.mean(x * x, axis=-1, keepdims=True) + RMS_EPS) * g


def _sigmoid(x):
    return 1.0 / (1.0 + jnp.exp(-x))


def _rope_kernel(pos_ref, invf_ref, cos_ref, sin_ref):
    pos = pos_ref[0].astype(F32)
    n_freq = HEAD_DIM // 2
    lane = lax.broadcasted_iota(jnp.int32, (pos.shape[0], LANES), 1)
    pos_l = pos[:, 3:4]
    for j in (2, 1, 0):
        pos_l = jnp.where(lane < (j + 1) * n_freq, pos[:, j:j + 1], pos_l)
    ang = pos_l * invf_ref[...]
    cos_ref[0] = jnp.cos(ang)
    sin_ref[0] = jnp.sin(ang)


def _rope_tables(positions):
    B, S = positions.shape
    n_freq = HEAD_DIM // 2
    per_row = LANES // n_freq
    rows = S // per_row
    inv_freq = 1.0 / (ROPE_THETA ** (jnp.arange(0, HEAD_DIM, 2, dtype=F32) / HEAD_DIM))
    invf = jnp.tile(inv_freq, per_row)[None, :]
    cos_c, sin_c = pl.pallas_call(
        _rope_kernel,
        grid=(B,),
        in_specs=[pl.BlockSpec((1, rows, per_row), lambda b: (b, 0, 0)),
                  pl.BlockSpec((1, LANES), lambda b: (0, 0))],
        out_specs=[pl.BlockSpec((1, rows, LANES), lambda b: (b, 0, 0))] * 2,
        out_shape=[jax.ShapeDtypeStruct((B, rows, LANES), F32)] * 2,
        name="rope_tables",
    )(positions.reshape(B, rows, per_row), invf)
    cos_c = cos_c.reshape(B, S, n_freq)
    sin_c = sin_c.reshape(B, S, n_freq)
    cos = jnp.concatenate([cos_c] * per_row, axis=-1)
    sin = jnp.concatenate([-sin_c, -sin_c, sin_c, sin_c], axis=-1)
    return cos, sin


def _deinterleave(ref, lead, d, rows):
    parts = [ref[lead + (pl.ds(r, rows // d, stride=d), slice(None))] for r in range(d)]
    return jnp.concatenate(parts, axis=0)


def _attn_in_kernel(x_ref, cos_ref, sin_ref, g_ref, w_ref, *refs, tm):
    out_refs = refs[:10]
    hs_ref = refs[10]
    qkv_refs = [out_refs[3 * g:3 * g + 3] for g in range(N_GROUPS)]
    zg_ref = out_refs[9]

    hn = _rmsnorm(x_ref[0], g_ref[...])
    for c in range(N_PAIRS):
        hs_ref[c] = hn[:, c * LANES:(c + 1) * LANES]
    h0 = hn.astype(BF16)

    for g, d in enumerate(DILATIONS):
        if d == 1:
            hg, cg, sg = h0, cos_ref[0], sin_ref[0]
        else:
            hg = jnp.concatenate(
                [_deinterleave(hs_ref, (c,), d, tm) for c in range(N_PAIRS)], axis=1).astype(BF16)
            cg = _deinterleave(cos_ref, (0,), d, tm)
            sg = _deinterleave(sin_ref, (0,), d, tm)
        for part in range(3):
            col0 = (part * N_GROUPS + g) * ATTN_WIDTH
            for cc in range(ATTN_WIDTH // 256):
                acc = jnp.dot(hg, w_ref[:, col0 + cc * 256:col0 + (cc + 1) * 256],
                              preferred_element_type=F32)
                for half in range(2):
                    t = acc[:, half * LANES:(half + 1) * LANES]
                    if part < 2:
                        t = t * cg + pltpu.roll(t, LANES // 2, axis=1) * sg
                    if part == 0:
                        t = t * QK_SCALE
                    qkv_refs[g][part][0, cc * 2 + half] = (
                        t.reshape(d, tm // d, LANES).astype(BF16))
    zc0 = 3 * N_GROUPS * ATTN_WIDTH
    for cc in range(ATTN_WIDTH // 256):
        z = jnp.dot(h0, w_ref[:, zc0 + cc * 256:zc0 + (cc + 1) * 256],
                    preferred_element_type=F32)
        zg = z * _sigmoid(z)
        for half in range(2):
            zg_ref[0, cc * 2 + half] = zg[:, half * LANES:(half + 1) * LANES].astype(BF16)


def _attn_in_proj(x, cos, sin, g, w):
    B, S, _ = x.shape
    tm = 256
    ncols = w.shape[1]
    out_shape, out_specs = [], []
    for d in DILATIONS:
        L = S // d
        for _ in range(3):
            out_shape.append(jax.ShapeDtypeStruct((B, N_PAIRS, d, L, LANES), BF16))
            out_specs.append(pl.BlockSpec((1, N_PAIRS, d, tm // d, LANES),
                                          lambda b, i: (b, 0, 0, i, 0)))
    out_shape.append(jax.ShapeDtypeStruct((B, N_PAIRS, S, LANES), BF16))
    out_specs.append(pl.BlockSpec((1, N_PAIRS, tm, LANES), lambda b, i: (b, 0, i, 0)))
    return pl.pallas_call(
        functools.partial(_attn_in_kernel, tm=tm),
        grid=(B, S // tm),
        in_specs=[pl.BlockSpec((1, tm, D_MODEL), lambda b, i: (b, i, 0)),
                  pl.BlockSpec((1, tm, LANES), lambda b, i: (b, i, 0)),
                  pl.BlockSpec((1, tm, LANES), lambda b, i: (b, i, 0)),
                  _const_spec((1, D_MODEL)),
                  _const_spec((D_MODEL, ncols))],
        out_specs=out_specs,
        out_shape=out_shape,
        scratch_shapes=[pltpu.VMEM((N_PAIRS, tm, LANES), F32)],
        compiler_params=pltpu.CompilerParams(
            dimension_semantics=("parallel", "parallel"), vmem_limit_bytes=VMEM_LIMIT),
        name="attn_in_proj",
    )(x, cos, sin, g, w)


def _attn_kernel(*refs, S, unroll):
    qkv_refs = [refs[3 * g:3 * g + 3] for g in range(N_GROUPS)]
    zg_ref, bias_ref, o_ref = refs[9:12]
    out01_scr, lse01_scr, out2_scr, lse2_scr, p_scr, m_scr = refs[12:18]
    T = N_BACK
    n_steps = (S // T) // unroll

    lane = lax.broadcasted_iota(jnp.int32, (T, LANES), 1)
    q_is_h0 = (lane // 32) % 2 == 0
    v_is_h0 = lane < HEAD_DIM
    ones = jnp.ones((2 * T, LANES), BF16)

    def block_coords(g, idx):
        nb = (S // DILATIONS[g]) // T
        r = idx // nb
        n = idx % nb
        start = pl.multiple_of(n * T, T)
        kstart = pl.multiple_of(jnp.maximum(start - T, 0), T)
        return r, n, start, kstart

    def stage_a(g, step, slot):
        q_ref, k_ref, _ = qkv_refs[g]
        for u in range(unroll):
            r, n, start, kstart = block_coords(g, step * unroll + u)
            q = q_ref[0, 0, r, pl.ds(start, T), :]
            k = k_ref[0, 0, r, pl.ds(kstart, 2 * T), :]
            zero = jnp.zeros_like(q)
            qq = jnp.concatenate(
                [jnp.where(q_is_h0, q, zero), jnp.where(q_is_h0, zero, q)], axis=0)
            s = lax.dot_general(qq, k, (((1,), (1,)), ((), ())), preferred_element_type=F32)
            s = s + bias_ref[jnp.where(n == 0, 1, 0)]
            m = jnp.max(s, axis=-1, keepdims=True)
            p_scr[slot, u] = jnp.exp(s - m).astype(BF16)
            m_scr[slot, u] = jnp.where(v_is_h0, m[:T], m[T:])

    def stage_b(g, step, slot):
        d = DILATIONS[g]
        v_ref = qkv_refs[g][2]
        for u in range(unroll):
            r, n, start, kstart = block_coords(g, step * unroll + u)
            v = v_ref[0, 0, r, pl.ds(kstart, 2 * T), :]
            res = jnp.dot(p_scr[slot, u], jnp.concatenate([v, ones], axis=1),
                          preferred_element_type=F32)
            acc = jnp.where(v_is_h0, res[:T, :LANES], res[T:, :LANES])
            l = jnp.where(v_is_h0, res[:T, LANES:], res[T:, LANES:])
            out = acc * (1.0 / l)
            lse = m_scr[slot, u] + jnp.log(l)
            if d == 1:
                out01_scr[0, pl.ds(start, T), :] = out
                lse01_scr[0, pl.ds(start, T), :] = lse
            elif d == 4:
                rows = pl.ds(start * d + r, T, stride=d)
                out01_scr[1, rows, :] = out
                lse01_scr[1, rows, :] = lse
            else:
                rows = pl.ds(start * PITCH16 + r, T, stride=PITCH16)
                out2_scr[rows, :] = out
                lse2_scr[rows, :] = lse

    stage_a(0, 0, 0)
    for g in range(N_GROUPS):
        base = g * n_steps

        def steady(i, carry, g=g, base=base):
            slot = (base + i) % 2
            stage_b(g, i, slot)
            stage_a(g, i + 1, 1 - slot)
            return carry

        lax.fori_loop(0, n_steps - 1, steady, 0)
        last_slot = (base + n_steps - 1) % 2
        stage_b(g, n_steps - 1, last_slot)
        if g + 1 < N_GROUPS:
            stage_a(g + 1, 0, 1 - last_slot)

    def combine(c, carry):
        rows = pl.ds(pl.multiple_of(c * T, T), T)
        base2 = pl.multiple_of(c * (T // 16) * PITCH16, 8)
        win2 = [pl.ds(base2 + j * PITCH16, 16) for j in range(T // 16)]
        lses = [lse01_scr[0, rows, :], lse01_scr[1, rows, :],
                jnp.concatenate([lse2_scr[w, :] for w in win2], axis=0)]
        outs = [out01_scr[0, rows, :], out01_scr[1, rows, :],
                jnp.concatenate([out2_scr[w, :] for w in win2], axis=0)]
        mx = jnp.maximum(jnp.maximum(lses[0], lses[1]), lses[2])
        num = jnp.zeros((T, LANES), F32)
        den = jnp.zeros((T, LANES), F32)
        for g in range(N_GROUPS):
            w = jnp.exp(lses[g] - mx)
            num = num + w * outs[g]
            den = den + w
        o_ref[0, 0, rows, :] = (num * (1.0 / den) * zg_ref[0, 0, rows, :].astype(F32)).astype(BF16)
        return carry

    lax.fori_loop(0, S // T, combine, 0)


def _band_bias():
    T = N_BACK
    row = np.arange(T)[:, None]
    col = np.arange(2 * T)[None, :]
    tables = []
    for off in (T, 0):
        dist = off + row - col
        b = np.where((dist >= 0) & (dist <= T), 0.0, NEG_INF).astype(np.float32)
        tables.append(np.concatenate([b, b], axis=0))
    return jnp.asarray(np.stack(tables))


def _attention(qkv, zg):
    B, _, S, _ = zg.shape
    T = N_BACK
    unroll = 16
    in_specs = []
    for d in DILATIONS:
        L = S // d
        in_specs += [pl.BlockSpec((1, 1, d, L, LANES), lambda b, p: (b, p, 0, 0, 0))] * 3
    in_specs.append(pl.BlockSpec((1, 1, S, LANES), lambda b, p: (b, p, 0, 0)))
    in_specs.append(_const_spec((2, 2 * T, 2 * T)))
    rows16 = S // 16 * PITCH16
    return pl.pallas_call(
        functools.partial(_attn_kernel, S=S, unroll=unroll),
        grid=(B, N_PAIRS),
        in_specs=in_specs,
        out_specs=pl.BlockSpec((1, 1, S, LANES), lambda b, p: (b, p, 0, 0)),
        out_shape=jax.ShapeDtypeStruct((B, N_PAIRS, S, LANES), BF16),
        scratch_shapes=[pltpu.VMEM((2, S, LANES), F32),
                        pltpu.VMEM((2, S, LANES), F32),
                        pltpu.VMEM((rows16, LANES), F32),
                        pltpu.VMEM((rows16, LANES), F32),
                        pltpu.VMEM((2, unroll, 2 * T, 2 * T), BF16),
                        pltpu.VMEM((2, unroll, T, LANES), F32)],
        compiler_params=pltpu.CompilerParams(
            dimension_semantics=("parallel", "parallel"), vmem_limit_bytes=VMEM_LIMIT),
        name="dilated_attention",
    )(*qkv, zg, _band_bias())


def _layer_tail(y, x, p_bf16, post_g, w_proj_ref, w_gate_ref):
    x1 = x + _rmsnorm(y, post_g)
    gate = _sigmoid(jnp.dot(x1.astype(BF16), w_gate_ref[...], preferred_element_type=F32))
    pe = jnp.dot(p_bf16, w_proj_ref[...], preferred_element_type=F32)
    return x1 + pe * gate


def _attn_out_kernel(o_ref, x_ref, p_ref, g_ref, wo_ref, wp_ref, wg_ref, out_ref):
    o = jnp.concatenate([o_ref[0, c] for c in range(N_PAIRS)], axis=1)
    y = jnp.dot(o, wo_ref[...], preferred_element_type=F32)
    out_ref[0] = _layer_tail(y, x_ref[0], p_ref[0, 0].astype(BF16), g_ref[...], wp_ref, wg_ref)


def _attn_out_proj(og, x, p, layer, post_g, w_out, w_proj, w_gate):
    B, S, _ = x.shape
    tm = 512
    return pl.pallas_call(
        _attn_out_kernel,
        grid=(B, S // tm),
        in_specs=[pl.BlockSpec((1, N_PAIRS, tm, LANES), lambda b, i: (b, 0, i, 0)),
                  pl.BlockSpec((1, tm, D_MODEL), lambda b, i: (b, i, 0)),
                  pl.BlockSpec((1, 1, tm, PLE_DIM), lambda b, i: (layer, b, i, 0)),
                  _const_spec((1, D_MODEL)),
                  _const_spec((ATTN_WIDTH, D_MODEL)),
                  _const_spec((PLE_DIM, D_MODEL)),
                  _const_spec((D_MODEL, D_MODEL))],
        out_specs=pl.BlockSpec((1, tm, D_MODEL), lambda b, i: (b, i, 0)),
        out_shape=jax.ShapeDtypeStruct((B, S, D_MODEL), F32),
        compiler_params=pltpu.CompilerParams(
            dimension_semantics=("parallel", "parallel"), vmem_limit_bytes=VMEM_LIMIT),
        name="attn_out_proj",
    )(og, x, p, post_g, w_out, w_proj, w_gate)


def _conv_kernel(xh_ref, x_ref, p_ref, pre_g_ref, w_in_ref, dw_w_ref, dw_b_ref, ln_g_ref,
                 ln_b_ref, w_out_ref, post_g_ref, wp_ref, wg_ref, out_ref, u_scr, cv_scr, *, tm):
    C = D_MODEL
    x = x_ref[0]
    xa = jnp.concatenate([xh_ref[0], x], axis=0)
    h = _rmsnorm(xa, pre_g_ref[...]).astype(BF16)
    ab = jnp.dot(h, w_in_ref[:, :2 * C], preferred_element_type=F32)
    u = ab[:, :C] * _sigmoid(ab[:, C:])
    row = lax.broadcasted_iota(jnp.int32, u.shape, 0)
    u = jnp.where((row >= CONV_HALO) | (pl.program_id(1) > 0), u, 0.0)
    for c in range(C // LANES):
        u_scr[c] = u[:, c * LANES:(c + 1) * LANES]

    RB = 128
    first = CONV_HALO - (CONV_WIDTH - 1)

    def conv_rows(i, carry):
        r0 = pl.multiple_of(i * RB, RB)
        for c in range(C // LANES):
            lanes = slice(c * LANES, (c + 1) * LANES)
            win = u_scr.at[c, pl.ds(r0, RB + CONV_HALO), :]
            acc = jnp.broadcast_to(dw_b_ref[:, lanes], (RB, LANES))
            for w in range(CONV_WIDTH):
                acc = acc + win[pl.ds(first + w, RB), :] * dw_w_ref[pl.ds(w, 1), lanes]
            cv_scr[c, pl.ds(r0, RB), :] = acc
        return carry

    lax.fori_loop(0, tm // RB, conv_rows, 0)

    z = jnp.dot(h[CONV_HALO:], w_in_ref[:, 2 * C:], preferred_element_type=F32)
    cv = jnp.concatenate([cv_scr[c] for c in range(C // LANES)], axis=1)
    mu = jnp.mean(cv, axis=-1, keepdims=True)
    cen = cv - mu
    var = jnp.mean(cen * cen, axis=-1, keepdims=True)
    y = cen * lax.rsqrt(var + LN_EPS) * ln_g_ref[...] + ln_b_ref[...]
    y = y * _sigmoid(y)
    yg = (y * (z * _sigmoid(z))).astype(BF16)
    out = jnp.dot(yg, w_out_ref[...], preferred_element_type=F32)
    out_ref[0] = _layer_tail(out, x, p_ref[0, 0].astype(BF16), post_g_ref[...], wp_ref, wg_ref)


def _conv_layer(x, p, layer, pre_g, w_in, dw_w, dw_b, ln_g, ln_b, w_out, post_g, w_proj, w_gate):
    B, S, C = x.shape
    tm = 512
    hb = tm // CONV_HALO
    return pl.pallas_call(
        functools.partial(_conv_kernel, tm=tm),
        grid=(B, S // tm),
        in_specs=[pl.BlockSpec((1, CONV_HALO, C), lambda b, i: (b, jnp.maximum(i * hb - 1, 0), 0)),
                  pl.BlockSpec((1, tm, C), lambda b, i: (b, i, 0)),
                  pl.BlockSpec((1, 1, tm, PLE_DIM), lambda b, i: (layer, b, i, 0)),
                  _const_spec((1, C)),
                  _const_spec((C, 3 * C)),
                  _const_spec((CONV_WIDTH, C)),
                  _const_spec((1, C)),
                  _const_spec((1, C)),
                  _const_spec((1, C)),
                  _const_spec((C, C)),
                  _const_spec((1, C)),
                  _const_spec((PLE_DIM, C)),
                  _const_spec((C, C))],
        out_specs=pl.BlockSpec((1, tm, C), lambda b, i: (b, i, 0)),
        out_shape=jax.ShapeDtypeStruct((B, S, C), F32),
        scratch_shapes=[pltpu.VMEM((C // LANES, tm + CONV_HALO, LANES), F32),
                        pltpu.VMEM((C // LANES, tm, LANES), F32)],
        compiler_params=pltpu.CompilerParams(
            dimension_semantics=("parallel", "parallel"), vmem_limit_bytes=VMEM_LIMIT),
        name="conv_layer",
    )(x, x, p, pre_g, w_in, dw_w, dw_b, ln_g, ln_b, w_out, post_g, w_proj, w_gate)


def _permute_qk_columns(w):
    n_qk = 2 * N_GROUPS * ATTN_WIDTH
    qk = w[:, :n_qk].reshape(w.shape[0], n_qk // LANES, 2, 2, HEAD_DIM // 2)
    qk = qk.transpose(0, 1, 3, 2, 4).reshape(w.shape[0], n_qk)
    return jnp.concatenate([qk, w[:, n_qk:]], axis=1)


def kernel(x, p, positions, pre_norm_g, post_norm_g, attn_w_in, attn_w_out, conv_w_in, conv_dw_w,
           conv_dw_b, conv_ln_g, conv_ln_b, conv_w_out, ple_w_proj, ple_w_gate):
    depth = p.shape[0]
    cos, sin = _rope_tables(positions)
    for i in range(depth):
        j = i // 2
        pre_g = pre_norm_g[i][None, :]
        post_g = post_norm_g[i][None, :]
        w_proj = ple_w_proj[i].astype(BF16)
        w_gate = ple_w_gate[i].astype(BF16)
        if i % 2 == 0:
            w_in = _permute_qk_columns(attn_w_in[j].astype(BF16))
            outs = _attn_in_proj(x, cos, sin, pre_g, w_in)
            og = _attention(outs[:9], outs[9])
            x = _attn_out_proj(og, x, p, i, post_g, attn_w_out[j].astype(BF16), w_proj, w_gate)
        else:
            x = _conv_layer(x, p, i, pre_g, conv_w_in[j].astype(BF16), conv_dw_w[j],
                            conv_dw_b[j][None, :], conv_ln_g[j][None, :], conv_ln_b[j][None, :],
                            conv_w_out[j].astype(BF16), post_g, w_proj, w_gate)
    return x
```

```python
import functools

import jax
import jax.numpy as jnp
import numpy as np
from jax import lax
from jax.experimental import pallas as pl
from jax.experimental.pallas import tpu as pltpu

D_MODEL = 1024
PLE_DIM = 256
HEAD_DIM = 64
N_HEADS = 16
N_PAIRS = N_HEADS // 2
ATTN_WIDTH = N_HEADS * HEAD_DIM
DILATIONS = (1, 4, 16)
N_BACK = 128
N_GROUPS = 3
ROPE_THETA = 10000.0
CONV_WIDTH = 31
CONV_HALO = 32
RMS_EPS = 1e-6
LN_EPS = 1e-5
NEG_INF = -1e30
LANES = 128
QK_SCALE = HEAD_DIM ** -0.5
PITCH16 = 24

F32 = jnp.float32
BF16 = jnp.bfloat16

VMEM_LIMIT = 56 * 1024 * 1024


def _const_spec(shape):
    return pl.BlockSpec(shape, lambda *_: (0,) * len(shape), pipeline_mode=pl.Buffered(1))


def _rmsnorm(x, g):
    return x * lax.rsqrt(jnp.mean(x * x, axis=-1, keepdims=True) + RMS_EPS) * g


def _sigmoid(x):
    return 1.0 / (1.0 + jnp.exp(-x))


N_FREQ = HEAD_DIM // 2
POS_PER_ROW = LANES // N_FREQ


def _rope_kernel(pos_ref, invf_ref, cos_ref, sin_ref):
    pos = pos_ref[0].astype(F32)
    rows = pos.shape[0]
    lane = lax.broadcasted_iota(jnp.int32, (rows, LANES), 1)
    pos_l = pos[:, POS_PER_ROW - 1:POS_PER_ROW]
    for j in range(POS_PER_ROW - 2, -1, -1):
        pos_l = jnp.where(lane < (j + 1) * N_FREQ, pos[:, j:j + 1], pos_l)
    ang = pos_l * invf_ref[...]
    tables = (jnp.cos(ang), jnp.sin(ang))
    is_lo = lane % HEAD_DIM < N_FREQ

    def spread(tab, j):
        base = tab if j == 0 else pltpu.roll(tab, LANES - j * N_FREQ, axis=1)
        out = base
        for g in range(1, POS_PER_ROW):
            out = jnp.where(lane >= g * N_FREQ, pltpu.roll(base, g * N_FREQ, axis=1), out)
        return out

    for j in range(POS_PER_ROW):
        rows_j = pl.ds(j, rows, stride=POS_PER_ROW)
        cos_ref[0, rows_j, :] = spread(tables[0], j)
        sin_j = spread(tables[1], j)
        sin_ref[0, rows_j, :] = jnp.where(is_lo, -sin_j, sin_j)


def _rope_tables(positions):
    B, S = positions.shape
    rows = S // POS_PER_ROW
    inv_freq = 1.0 / (ROPE_THETA ** (jnp.arange(0, HEAD_DIM, 2, dtype=F32) / HEAD_DIM))
    invf = jnp.tile(inv_freq, POS_PER_ROW)[None, :]
    return pl.pallas_call(
        _rope_kernel,
        grid=(B,),
        in_specs=[pl.BlockSpec((1, rows, POS_PER_ROW), lambda b: (b, 0, 0)),
                  pl.BlockSpec((1, LANES), lambda b: (0, 0))],
        out_specs=[pl.BlockSpec((1, S, LANES), lambda b: (b, 0, 0))] * 2,
        out_shape=[jax.ShapeDtypeStruct((B, S, LANES), F32)] * 2,
        name="rope_tables",
    )(positions.reshape(B, rows, POS_PER_ROW), invf)


def _deinterleave(ref, lead, d, rows):
    parts = [ref[lead + (pl.ds(r, rows // d, stride=d), slice(None))] for r in range(d)]
    return jnp.concatenate(parts, axis=0)


def _attn_in_kernel(x_ref, cos_ref, sin_ref, g_ref, w_ref, *refs, tm):
    out_refs = refs[:10]
    hs_ref = refs[10]
    qkv_refs = [out_refs[3 * g:3 * g + 3] for g in range(N_GROUPS)]
    zg_ref = out_refs[9]

    hn = _rmsnorm(x_ref[0], g_ref[...])
    for c in range(N_PAIRS):
        hs_ref[c] = hn[:, c * LANES:(c + 1) * LANES]
    h0 = hn.astype(BF16)
    lane = lax.broadcasted_iota(jnp.int32, (tm, LANES), 1)
    is_lo = lane % HEAD_DIM < N_FREQ

    for g, d in enumerate(DILATIONS):
        if d == 1:
            hg, cg, sg = h0, cos_ref[0], sin_ref[0]
        else:
            hg = jnp.concatenate(
                [_deinterleave(hs_ref, (c,), d, tm) for c in range(N_PAIRS)], axis=1).astype(BF16)
            cg = _deinterleave(cos_ref, (0,), d, tm)
            sg = _deinterleave(sin_ref, (0,), d, tm)
        for part in range(3):
            col0 = (part * N_GROUPS + g) * ATTN_WIDTH
            for cc in range(ATTN_WIDTH // 256):
                acc = jnp.dot(hg, w_ref[:, col0 + cc * 256:col0 + (cc + 1) * 256],
                              preferred_element_type=F32)
                for half in range(2):
                    t = acc[:, half * LANES:(half + 1) * LANES]
                    if part < 2:
                        rot = jnp.where(is_lo, pltpu.roll(t, LANES - N_FREQ, axis=1),
                                        pltpu.roll(t, N_FREQ, axis=1))
                        t = t * cg + rot * sg
                    if part == 0:
                        t = t * QK_SCALE
                    qkv_refs[g][part][0, cc * 2 + half] = (
                        t.reshape(d, tm // d, LANES).astype(BF16))
    zc0 = 3 * N_GROUPS * ATTN_WIDTH
    for cc in range(ATTN_WIDTH // 256):
        z = jnp.dot(h0, w_ref[:, zc0 + cc * 256:zc0 + (cc + 1) * 256],
                    preferred_element_type=F32)
        zg = z * _sigmoid(z)
        for half in range(2):
            zg_ref[0, cc * 2 + half] = zg[:, half * LANES:(half + 1) * LANES].astype(BF16)


def _attn_in_proj(x, cos, sin, g, w):
    B, S, _ = x.shape
    tm = 256
    ncols = w.shape[1]
    out_shape, out_specs = [], []
    for d in DILATIONS:
        L = S // d
        for _ in range(3):
            out_shape.append(jax.ShapeDtypeStruct((B, N_PAIRS, d, L, LANES), BF16))
            out_specs.append(pl.BlockSpec((1, N_PAIRS, d, tm // d, LANES),
                                          lambda b, i: (b, 0, 0, i, 0)))
    out_shape.append(jax.ShapeDtypeStruct((B, N_PAIRS, S, LANES), BF16))
    out_specs.append(pl.BlockSpec((1, N_PAIRS, tm, LANES), lambda b, i: (b, 0, i, 0)))
    return pl.pallas_call(
        functools.partial(_attn_in_kernel, tm=tm),
        grid=(B, S // tm),
        in_specs=[pl.BlockSpec((1, tm, D_MODEL), lambda b, i: (b, i, 0)),
                  pl.BlockSpec((1, tm, LANES), lambda b, i: (b, i, 0)),
                  pl.BlockSpec((1, tm, LANES), lambda b, i: (b, i, 0)),
                  _const_spec((1, D_MODEL)),
                  _const_spec((D_MODEL, ncols))],
        out_specs=out_specs,
        out_shape=out_shape,
        scratch_shapes=[pltpu.VMEM((N_PAIRS, tm, LANES), F32)],
        compiler_params=pltpu.CompilerParams(
            dimension_semantics=("parallel", "parallel"), vmem_limit_bytes=VMEM_LIMIT),
        name="attn_in_proj",
    )(x, cos, sin, g, w)


def _attn_kernel(*refs, S, unroll):
    qkv_refs = [refs[3 * g:3 * g + 3] for g in range(N_GROUPS)]
    zg_ref, bias_ref, o_ref = refs[9:12]
    out01_scr, lse01_scr, out2_scr, lse2_scr, p_scr, m_scr = refs[12:18]
    T = N_BACK
    n_steps = (S // T) // unroll

    lane = lax.broadcasted_iota(jnp.int32, (T, LANES), 1)
    v_is_h0 = lane < HEAD_DIM
    q_is_h0 = v_is_h0
    ones = jnp.ones((2 * T, LANES), BF16)

    def block_coords(g, idx):
        nb = (S // DILATIONS[g]) // T
        r = idx // nb
        n = idx % nb
        start = pl.multiple_of(n * T, T)
        kstart = pl.multiple_of(jnp.maximum(start - T, 0), T)
        return r, n, start, kstart

    def stage_a(g, step, slot):
        q_ref, k_ref, _ = qkv_refs[g]
        for u in range(unroll):
            r, n, start, kstart = block_coords(g, step * unroll + u)
            q = q_ref[0, 0, r, pl.ds(start, T), :]
            k = k_ref[0, 0, r, pl.ds(kstart, 2 * T), :]
            zero = jnp.zeros_like(q)
            qq = jnp.concatenate(
                [jnp.where(q_is_h0, q, zero), jnp.where(q_is_h0, zero, q)], axis=0)
            s = lax.dot_general(qq, k, (((1,), (1,)), ((), ())), preferred_element_type=F32)
            s = s + bias_ref[jnp.where(n == 0, 1, 0)]
            m = jnp.max(s, axis=-1, keepdims=True)
            p_scr[slot, u] = jnp.exp(s - m).astype(BF16)
            m_scr[slot, u] = jnp.where(v_is_h0, m[:T], m[T:])

    def stage_b(g, step, slot):
        d = DILATIONS[g]
        v_ref = qkv_refs[g][2]
        for u in range(unroll):
            r, n, start, kstart = block_coords(g, step * unroll + u)
            v = v_ref[0, 0, r, pl.ds(kstart, 2 * T), :]
            res = jnp.dot(p_scr[slot, u], jnp.concatenate([v, ones], axis=1),
                          preferred_element_type=F32)
            acc = jnp.where(v_is_h0, res[:T, :LANES], res[T:, :LANES])
            l = jnp.where(v_is_h0, res[:T, LANES:], res[T:, LANES:])
            out = acc * (1.0 / l)
            lse = m_scr[slot, u] + jnp.log(l)
            if d == 1:
                out01_scr[0, pl.ds(start, T), :] = out
                lse01_scr[0, pl.ds(start, T), :] = lse
            elif d == 4:
                rows = pl.ds(start * d + r, T, stride=d)
                out01_scr[1, rows, :] = out
                lse01_scr[1, rows, :] = lse
            else:
                rows = pl.ds(start * PITCH16 + r, T, stride=PITCH16)
                out2_scr[rows, :] = out
                lse2_scr[rows, :] = lse

    stage_a(0, 0, 0)
    for g in range(N_GROUPS):
        base = g * n_steps

        def steady(i, carry, g=g, base=base):
            slot = (base + i) % 2
            stage_b(g, i, slot)
            stage_a(g, i + 1, 1 - slot)
            return carry

        lax.fori_loop(0, n_steps - 1, steady, 0)
        last_slot = (base + n_steps - 1) % 2
        stage_b(g, n_steps - 1, last_slot)
        if g + 1 < N_GROUPS:
            stage_a(g + 1, 0, 1 - last_slot)

    def combine(c, carry):
        rows = pl.ds(pl.multiple_of(c * T, T), T)
        base2 = pl.multiple_of(c * (T // 16) * PITCH16, 8)
        win2 = [pl.ds(base2 + j * PITCH16, 16) for j in range(T // 16)]
        lses = [lse01_scr[0, rows, :], lse01_scr[1, rows, :],
                jnp.concatenate([lse2_scr[w, :] for w in win2], axis=0)]
        outs = [out01_scr[0, rows, :], out01_scr[1, rows, :],
                jnp.concatenate([out2_scr[w, :] for w in win2], axis=0)]
        mx = jnp.maximum(jnp.maximum(lses[0], lses[1]), lses[2])
        num = jnp.zeros((T, LANES), F32)
        den = jnp.zeros((T, LANES), F32)
        for g in range(N_GROUPS):
            w = jnp.exp(lses[g] - mx)
            num = num + w * outs[g]
            den = den + w
        o_ref[0, 0, rows, :] = (num * (1.0 / den) * zg_ref[0, 0, rows, :].astype(F32)).astype(BF16)
        return carry

    lax.fori_loop(0, S // T, combine, 0)


def _band_bias():
    T = N_BACK
    row = np.arange(T)[:, None]
    col = np.arange(2 * T)[None, :]
    tables = []
    for off in (T, 0):
        dist = off + row - col
        b = np.where((dist >= 0) & (dist <= T), 0.0, NEG_INF).astype(np.float32)
        tables.append(np.concatenate([b, b], axis=0))
    return jnp.asarray(np.stack(tables))


def _attention(qkv, zg):
    B, _, S, _ = zg.shape
    T = N_BACK
    unroll = 16
    in_specs = []
    for d in DILATIONS:
        L = S // d
        in_specs += [pl.BlockSpec((1, 1, d, L, LANES), lambda b, p: (b, p, 0, 0, 0))] * 3
    in_specs.append(pl.BlockSpec((1, 1, S, LANES), lambda b, p: (b, p, 0, 0)))
    in_specs.append(_const_spec((2, 2 * T, 2 * T)))
    rows16 = S // 16 * PITCH16
    return pl.pallas_call(
        functools.partial(_attn_kernel, S=S, unroll=unroll),
        grid=(B, N_PAIRS),
        in_specs=in_specs,
        out_specs=pl.BlockSpec((1, 1, S, LANES), lambda b, p: (b, p, 0, 0)),
        out_shape=jax.ShapeDtypeStruct((B, N_PAIRS, S, LANES), BF16),
        scratch_shapes=[pltpu.VMEM((2, S, LANES), F32),
                        pltpu.VMEM((2, S, LANES), F32),
                        pltpu.VMEM((rows16, LANES), F32),
                        pltpu.VMEM((rows16, LANES), F32),
                        pltpu.VMEM((2, unroll, 2 * T, 2 * T), BF16),
                        pltpu.VMEM((2, unroll, T, LANES), F32)],
        compiler_params=pltpu.CompilerParams(
            dimension_semantics=("parallel", "parallel"), vmem_limit_bytes=VMEM_LIMIT),
        name="dilated_attention",
    )(*qkv, zg, _band_bias())


def _layer_tail(y, x, p_bf16, post_g, w_proj_ref, w_gate_ref):
    x1 = x + _rmsnorm(y, post_g)
    gate = _sigmoid(jnp.dot(x1.astype(BF16), w_gate_ref[...], preferred_element_type=F32))
    pe = jnp.dot(p_bf16, w_proj_ref[...], preferred_element_type=F32)
    return x1 + pe * gate


def _attn_out_kernel(o_ref, x_ref, p_ref, g_ref, wo_ref, wp_ref, wg_ref, out_ref):
    o = jnp.concatenate([o_ref[0, c] for c in range(N_PAIRS)], axis=1)
    y = jnp.dot(o, wo_ref[...], preferred_element_type=F32)
    out_ref[0] = _layer_tail(y, x_ref[0], p_ref[0, 0].astype(BF16), g_ref[...], wp_ref, wg_ref)


def _attn_out_proj(og, x, p, layer, post_g, w_out, w_proj, w_gate):
    B, S, _ = x.shape
    tm = 512
    return pl.pallas_call(
        _attn_out_kernel,
        grid=(B, S // tm),
        in_specs=[pl.BlockSpec((1, N_PAIRS, tm, LANES), lambda b, i: (b, 0, i, 0)),
                  pl.BlockSpec((1, tm, D_MODEL), lambda b, i: (b, i, 0)),
                  pl.BlockSpec((1, 1, tm, PLE_DIM), lambda b, i: (layer, b, i, 0)),
                  _const_spec((1, D_MODEL)),
                  _const_spec((ATTN_WIDTH, D_MODEL)),
                  _const_spec((PLE_DIM, D_MODEL)),
                  _const_spec((D_MODEL, D_MODEL))],
        out_specs=pl.BlockSpec((1, tm, D_MODEL), lambda b, i: (b, i, 0)),
        out_shape=jax.ShapeDtypeStruct((B, S, D_MODEL), F32),
        compiler_params=pltpu.CompilerParams(
            dimension_semantics=("parallel", "parallel"), vmem_limit_bytes=VMEM_LIMIT),
        name="attn_out_proj",
    )(og, x, p, post_g, w_out, w_proj, w_gate)


def _conv_kernel(xh_ref, x_ref, p_ref, pre_g_ref, w_in_ref, dw_w_ref, dw_b_ref, ln_g_ref,
                 ln_b_ref, w_out_ref, post_g_ref, wp_ref, wg_ref, out_ref, u_scr, cv_scr, *, tm):
    C = D_MODEL
    x = x_ref[0]
    xa = jnp.concatenate([xh_ref[0], x], axis=0)
    h = _rmsnorm(xa, pre_g_ref[...]).astype(BF16)
    ab = jnp.dot(h, w_in_ref[:, :2 * C], preferred_element_type=F32)
    u = ab[:, :C] * _sigmoid(ab[:, C:])
    row = lax.broadcasted_iota(jnp.int32, u.shape, 0)
    u = jnp.where((row >= CONV_HALO) | (pl.program_id(1) > 0), u, 0.0)
    for c in range(C // LANES):
        u_scr[c] = u[:, c * LANES:(c + 1) * LANES]

    RB = 128
    first = CONV_HALO - (CONV_WIDTH - 1)

    def conv_rows(i, carry):
        r0 = pl.multiple_of(i * RB, RB)
        for c in range(C // LANES):
            lanes = slice(c * LANES, (c + 1) * LANES)
            win = u_scr.at[c, pl.ds(r0, RB + CONV_HALO), :]
            acc = jnp.broadcast_to(dw_b_ref[:, lanes], (RB, LANES))
            for w in range(CONV_WIDTH):
                acc = acc + win[pl.ds(first + w, RB), :] * dw_w_ref[pl.ds(w, 1), lanes]
            cv_scr[c, pl.ds(r0, RB), :] = acc
        return carry

    lax.fori_loop(0, tm // RB, conv_rows, 0)

    z = jnp.dot(h[CONV_HALO:], w_in_ref[:, 2 * C:], preferred_element_type=F32)
    cv = jnp.concatenate([cv_scr[c] for c in range(C // LANES)], axis=1)
    mu = jnp.mean(cv, axis=-1, keepdims=True)
    cen = cv - mu
    var = jnp.mean(cen * cen, axis=-1, keepdims=True)
    y = cen * lax.rsqrt(var + LN_EPS) * ln_g_ref[...] + ln_b_ref[...]
    y = y * _sigmoid(y)
    yg = (y * (z * _sigmoid(z))).astype(BF16)
    out = jnp.dot(yg, w_out_ref[...], preferred_element_type=F32)
    out_ref[0] = _layer_tail(out, x, p_ref[0, 0].astype(BF16), post_g_ref[...], wp_ref, wg_ref)


def _conv_layer(x, p, layer, pre_g, w_in, dw_w, dw_b, ln_g, ln_b, w_out, post_g, w_proj, w_gate):
    B, S, C = x.shape
    tm = 512
    hb = tm // CONV_HALO
    return pl.pallas_call(
        functools.partial(_conv_kernel, tm=tm),
        grid=(B, S // tm),
        in_specs=[pl.BlockSpec((1, CONV_HALO, C), lambda b, i: (b, jnp.maximum(i * hb - 1, 0), 0)),
                  pl.BlockSpec((1, tm, C), lambda b, i: (b, i, 0)),
                  pl.BlockSpec((1, 1, tm, PLE_DIM), lambda b, i: (layer, b, i, 0)),
                  _const_spec((1, C)),
                  _const_spec((C, 3 * C)),
                  _const_spec((CONV_WIDTH, C)),
                  _const_spec((1, C)),
                  _const_spec((1, C)),
                  _const_spec((1, C)),
                  _const_spec((C, C)),
                  _const_spec((1, C)),
                  _const_spec((PLE_DIM, C)),
                  _const_spec((C, C))],
        out_specs=pl.BlockSpec((1, tm, C), lambda b, i: (b, i, 0)),
        out_shape=jax.ShapeDtypeStruct((B, S, C), F32),
        scratch_shapes=[pltpu.VMEM((C // LANES, tm + CONV_HALO, LANES), F32),
                        pltpu.VMEM((C // LANES, tm, LANES), F32)],
        compiler_params=pltpu.CompilerParams(
            dimension_semantics=("parallel", "parallel"), vmem_limit_bytes=VMEM_LIMIT),
        name="conv_layer",
    )(x, x, p, pre_g, w_in, dw_w, dw_b, ln_g, ln_b, w_out, post_g, w_proj, w_gate)


def kernel(x, p, positions, pre_norm_g, post_norm_g, attn_w_in, attn_w_out, conv_w_in, conv_dw_w,
           conv_dw_b, conv_ln_g, conv_ln_b, conv_w_out, ple_w_proj, ple_w_gate):
    depth = p.shape[0]
    cos, sin = _rope_tables(positions)
    for i in range(depth):
        j = i // 2
        pre_g = pre_norm_g[i][None, :]
        post_g = post_norm_g[i][None, :]
        w_proj = ple_w_proj[i].astype(BF16)
        w_gate = ple_w_gate[i].astype(BF16)
        if i % 2 == 0:
            outs = _attn_in_proj(x, cos, sin, pre_g, attn_w_in[j].astype(BF16))
            og = _attention(outs[:9], outs[9])
            x = _attn_out_proj(og, x, p, i, post_g, attn_w_out[j].astype(BF16), w_proj, w_gate)
        else:
            x = _conv_layer(x, p, i, pre_g, conv_w_in[j].astype(BF16), conv_dw_w[j],
                            conv_dw_b[j][None, :], conv_ln_g[j][None, :], conv_ln_b[j][None, :],
                            conv_w_out[j].astype(BF16), post_g, w_proj, w_gate)
    return x
```

```python
import functools

import jax
import jax.numpy as jnp
import numpy as np
from jax import lax
from jax.experimental import pallas as pl
from jax.experimental.pallas import tpu as pltpu

D_MODEL = 1024
PLE_DIM = 256
HEAD_DIM = 64
N_HEADS = 16
N_PAIRS = N_HEADS // 2
ATTN_WIDTH = N_HEADS * HEAD_DIM
DILATIONS = (1, 4, 16)
N_BACK = 128
N_GROUPS = 3
ROPE_THETA = 10000.0
CONV_WIDTH = 31
CONV_HALO = 32
RMS_EPS = 1e-6
LN_EPS = 1e-5
NEG_INF = -1e30
LANES = 128
QK_SCALE = HEAD_DIM ** -0.5
PITCH16 = 20

F32 = jnp.float32
BF16 = jnp.bfloat16

VMEM_LIMIT = 56 * 1024 * 1024


def _const_spec(shape):
    return pl.BlockSpec(shape, lambda *_: (0,) * len(shape), pipeline_mode=pl.Buffered(1))


def _rmsnorm(x, g):
    return x * lax.rsqrt(jnp.mean(x * x, axis=-1, keepdims=True) + RMS_EPS) * g


def _sigmoid(x):
    return 1.0 / (1.0 + jnp.exp(-x))


N_FREQ = HEAD_DIM // 2
POS_PER_ROW = LANES // N_FREQ


def _rope_kernel(pos_ref, invf_ref, cos_ref, sin_ref):
    pos = pos_ref[0].astype(F32)
    rows = pos.shape[0]
    lane = lax.broadcasted_iota(jnp.int32, (rows, LANES), 1)
    pos_l = pos[:, POS_PER_ROW - 1:POS_PER_ROW]
    for j in range(POS_PER_ROW - 2, -1, -1):
        pos_l = jnp.where(lane < (j + 1) * N_FREQ, pos[:, j:j + 1], pos_l)
    ang = pos_l * invf_ref[...]
    tables = (jnp.cos(ang), jnp.sin(ang))
    is_lo = lane % HEAD_DIM < N_FREQ

    def spread(tab, j):
        base = tab if j == 0 else pltpu.roll(tab, LANES - j * N_FREQ, axis=1)
        out = base
        for g in range(1, POS_PER_ROW):
            out = jnp.where(lane >= g * N_FREQ, pltpu.roll(base, g * N_FREQ, axis=1), out)
        return out

    for j in range(POS_PER_ROW):
        rows_j = pl.ds(j, rows, stride=POS_PER_ROW)
        cos_ref[0, rows_j, :] = spread(tables[0], j)
        sin_j = spread(tables[1], j)
        sin_ref[0, rows_j, :] = jnp.where(is_lo, -sin_j, sin_j)


def _rope_tables(positions):
    B, S = positions.shape
    rows = S // POS_PER_ROW
    inv_freq = 1.0 / (ROPE_THETA ** (jnp.arange(0, HEAD_DIM, 2, dtype=F32) / HEAD_DIM))
    invf = jnp.tile(inv_freq, POS_PER_ROW)[None, :]
    return pl.pallas_call(
        _rope_kernel,
        grid=(B,),
        in_specs=[pl.BlockSpec((1, rows, POS_PER_ROW), lambda b: (b, 0, 0)),
                  pl.BlockSpec((1, LANES), lambda b: (0, 0))],
        out_specs=[pl.BlockSpec((1, S, LANES), lambda b: (b, 0, 0))] * 2,
        out_shape=[jax.ShapeDtypeStruct((B, S, LANES), F32)] * 2,
        name="rope_tables",
    )(positions.reshape(B, rows, POS_PER_ROW), invf)


def _deinterleave(ref, lead, d, rows):
    parts = [ref[lead + (pl.ds(r, rows // d, stride=d), slice(None))] for r in range(d)]
    return jnp.concatenate(parts, axis=0)


def _attn_in_kernel(x_ref, cos_ref, sin_ref, g_ref, w_ref, *refs, tm):
    out_refs = refs[:10]
    hs_ref = refs[10]
    qkv_refs = [out_refs[3 * g:3 * g + 3] for g in range(N_GROUPS)]
    zg_ref = out_refs[9]

    hn = _rmsnorm(x_ref[0], g_ref[...])
    for c in range(N_PAIRS):
        hs_ref[c] = hn[:, c * LANES:(c + 1) * LANES]
    h0 = hn.astype(BF16)
    lane = lax.broadcasted_iota(jnp.int32, (tm, LANES), 1)
    is_lo = lane % HEAD_DIM < N_FREQ

    for g, d in enumerate(DILATIONS):
        if d == 1:
            hg, cg, sg = h0, cos_ref[0], sin_ref[0]
        else:
            hg = jnp.concatenate(
                [_deinterleave(hs_ref, (c,), d, tm) for c in range(N_PAIRS)], axis=1).astype(BF16)
            cg = _deinterleave(cos_ref, (0,), d, tm)
            sg = _deinterleave(sin_ref, (0,), d, tm)
        for part in range(3):
            col0 = (part * N_GROUPS + g) * ATTN_WIDTH
            for cc in range(ATTN_WIDTH // 256):
                acc = jnp.dot(hg, w_ref[:, col0 + cc * 256:col0 + (cc + 1) * 256],
                              preferred_element_type=F32)
                for half in range(2):
                    t = acc[:, half * LANES:(half + 1) * LANES]
                    if part < 2:
                        rot = jnp.where(is_lo, pltpu.roll(t, LANES - N_FREQ, axis=1),
                                        pltpu.roll(t, N_FREQ, axis=1))
                        t = t * cg + rot * sg
                    if part == 0:
                        t = t * QK_SCALE
                    qkv_refs[g][part][0, cc * 2 + half] = (
                        t.reshape(d, tm // d, LANES).astype(BF16))
    zc0 = 3 * N_GROUPS * ATTN_WIDTH
    for cc in range(ATTN_WIDTH // 256):
        z = jnp.dot(h0, w_ref[:, zc0 + cc * 256:zc0 + (cc + 1) * 256],
                    preferred_element_type=F32)
        zg = z * _sigmoid(z)
        for half in range(2):
            zg_ref[0, cc * 2 + half] = zg[:, half * LANES:(half + 1) * LANES].astype(BF16)


def _attn_in_proj(x, cos, sin, g, w):
    B, S, _ = x.shape
    tm = 256
    ncols = w.shape[1]
    out_shape, out_specs = [], []
    for d in DILATIONS:
        L = S // d
        for _ in range(3):
            out_shape.append(jax.ShapeDtypeStruct((B, N_PAIRS, d, L, LANES), BF16))
            out_specs.append(pl.BlockSpec((1, N_PAIRS, d, tm // d, LANES),
                                          lambda b, i: (b, 0, 0, i, 0)))
    out_shape.append(jax.ShapeDtypeStruct((B, N_PAIRS, S, LANES), BF16))
    out_specs.append(pl.BlockSpec((1, N_PAIRS, tm, LANES), lambda b, i: (b, 0, i, 0)))
    return pl.pallas_call(
        functools.partial(_attn_in_kernel, tm=tm),
        grid=(B, S // tm),
        in_specs=[pl.BlockSpec((1, tm, D_MODEL), lambda b, i: (b, i, 0)),
                  pl.BlockSpec((1, tm, LANES), lambda b, i: (b, i, 0)),
                  pl.BlockSpec((1, tm, LANES), lambda b, i: (b, i, 0)),
                  _const_spec((1, D_MODEL)),
                  _const_spec((D_MODEL, ncols))],
        out_specs=out_specs,
        out_shape=out_shape,
        scratch_shapes=[pltpu.VMEM((N_PAIRS, tm, LANES), F32)],
        compiler_params=pltpu.CompilerParams(
            dimension_semantics=("parallel", "parallel"), vmem_limit_bytes=VMEM_LIMIT),
        name="attn_in_proj",
    )(x, cos, sin, g, w)


def _attn_kernel(*refs, S, unroll):
    qkv_refs = [refs[3 * g:3 * g + 3] for g in range(N_GROUPS)]
    zg_ref, bias_ref, o_ref = refs[9:12]
    out01_scr, lse01_scr, out2_scr, lse2_scr, p_scr, m_scr = refs[12:18]
    T = N_BACK
    n_steps = (S // T) // unroll

    lane = lax.broadcasted_iota(jnp.int32, (T, LANES), 1)
    v_is_h0 = lane < HEAD_DIM
    q_is_h0 = v_is_h0
    ones = jnp.ones((2 * T, LANES), BF16)

    def block_coords(g, idx):
        nb = (S // DILATIONS[g]) // T
        r = idx // nb
        n = idx % nb
        start = pl.multiple_of(n * T, T)
        kstart = pl.multiple_of(jnp.maximum(start - T, 0), T)
        return r, n, start, kstart

    def stage_a(g, step, slot):
        q_ref, k_ref, _ = qkv_refs[g]
        for u in range(unroll):
            r, n, start, kstart = block_coords(g, step * unroll + u)
            q = q_ref[0, 0, r, pl.ds(start, T), :]
            k = k_ref[0, 0, r, pl.ds(kstart, 2 * T), :]
            zero = jnp.zeros_like(q)
            qq = jnp.concatenate(
                [jnp.where(q_is_h0, q, zero), jnp.where(q_is_h0, zero, q)], axis=0)
            s = lax.dot_general(qq, k, (((1,), (1,)), ((), ())), preferred_element_type=F32)
            s = s + bias_ref[jnp.where(n == 0, 1, 0)]
            m = jnp.max(s, axis=-1, keepdims=True)
            p_scr[slot, u] = jnp.exp(s - m).astype(BF16)
            m_scr[slot, u] = jnp.where(v_is_h0, m[:T], m[T:])

    def stage_b(g, step, slot):
        d = DILATIONS[g]
        v_ref = qkv_refs[g][2]
        for u in range(unroll):
            r, n, start, kstart = block_coords(g, step * unroll + u)
            v = v_ref[0, 0, r, pl.ds(kstart, 2 * T), :]
            res = jnp.dot(p_scr[slot, u], jnp.concatenate([v, ones], axis=1),
                          preferred_element_type=F32)
            acc = jnp.where(v_is_h0, res[:T, :LANES], res[T:, :LANES])
            l = jnp.where(v_is_h0, res[:T, LANES:], res[T:, LANES:])
            out = acc * (1.0 / l)
            lse = m_scr[slot, u] + jnp.log(l)
            if d == 1:
                out01_scr[0, pl.ds(start, T), :] = out
                lse01_scr[0, pl.ds(start, T), :] = lse
            elif d == 4:
                rows = pl.ds(start * d + r, T, stride=d)
                out01_scr[1, rows, :] = out
                lse01_scr[1, rows, :] = lse
            else:
                rows = pl.ds(start * PITCH16 + r, T, stride=PITCH16)
                out2_scr[rows, :] = out
                lse2_scr[rows, :] = lse

    stage_a(0, 0, 0)
    for g in range(N_GROUPS):
        base = g * n_steps

        def steady(i, carry, g=g, base=base):
            slot = (base + i) % 2
            stage_b(g, i, slot)
            stage_a(g, i + 1, 1 - slot)
            return carry

        lax.fori_loop(0, n_steps - 1, steady, 0)
        last_slot = (base + n_steps - 1) % 2
        stage_b(g, n_steps - 1, last_slot)
        if g + 1 < N_GROUPS:
            stage_a(g + 1, 0, 1 - last_slot)

    def combine(c, carry):
        rows = pl.ds(pl.multiple_of(c * T, T), T)
        base2 = pl.multiple_of(c * (T // 16) * PITCH16, 8)
        win2 = [pl.ds(base2 + j * PITCH16, 16) for j in range(T // 16)]
        lses = [lse01_scr[0, rows, :], lse01_scr[1, rows, :],
                jnp.concatenate([lse2_scr[w, :] for w in win2], axis=0)]
        outs = [out01_scr[0, rows, :], out01_scr[1, rows, :],
                jnp.concatenate([out2_scr[w, :] for w in win2], axis=0)]
        mx = jnp.maximum(jnp.maximum(lses[0], lses[1]), lses[2])
        num = jnp.zeros((T, LANES), F32)
        den = jnp.zeros((T, LANES), F32)
        for g in range(N_GROUPS):
            w = jnp.exp(lses[g] - mx)
            num = num + w * outs[g]
            den = den + w
        o_ref[0, 0, rows, :] = (num * (1.0 / den) * zg_ref[0, 0, rows, :].astype(F32)).astype(BF16)
        return carry

    lax.fori_loop(0, S // T, combine, 0, unroll=2)


def _band_bias():
    T = N_BACK
    row = np.arange(T)[:, None]
    col = np.arange(2 * T)[None, :]
    tables = []
    for off in (T, 0):
        dist = off + row - col
        b = np.where((dist >= 0) & (dist <= T), 0.0, NEG_INF).astype(np.float32)
        tables.append(np.concatenate([b, b], axis=0))
    return jnp.asarray(np.stack(tables))


def _attention(qkv, zg):
    B, _, S, _ = zg.shape
    T = N_BACK
    unroll = 16
    in_specs = []
    for d in DILATIONS:
        L = S // d
        in_specs += [pl.BlockSpec((1, 1, d, L, LANES), lambda b, p: (b, p, 0, 0, 0))] * 3
    in_specs.append(pl.BlockSpec((1, 1, S, LANES), lambda b, p: (b, p, 0, 0)))
    in_specs.append(_const_spec((2, 2 * T, 2 * T)))
    rows16 = S // 16 * PITCH16
    return pl.pallas_call(
        functools.partial(_attn_kernel, S=S, unroll=unroll),
        grid=(B, N_PAIRS),
        in_specs=in_specs,
        out_specs=pl.BlockSpec((1, 1, S, LANES), lambda b, p: (b, p, 0, 0)),
        out_shape=jax.ShapeDtypeStruct((B, N_PAIRS, S, LANES), BF16),
        scratch_shapes=[pltpu.VMEM((2, S, LANES), F32),
                        pltpu.VMEM((2, S, LANES), F32),
                        pltpu.VMEM((rows16, LANES), F32),
                        pltpu.VMEM((rows16, LANES), F32),
                        pltpu.VMEM((2, unroll, 2 * T, 2 * T), BF16),
                        pltpu.VMEM((2, unroll, T, LANES), F32)],
        compiler_params=pltpu.CompilerParams(
            dimension_semantics=("parallel", "parallel"), vmem_limit_bytes=VMEM_LIMIT),
        name="dilated_attention",
    )(*qkv, zg, _band_bias())


def _layer_tail(y, x, p_bf16, post_g, w_proj_ref, w_gate_ref):
    x1 = x + _rmsnorm(y, post_g)
    gate = _sigmoid(jnp.dot(x1.astype(BF16), w_gate_ref[...], preferred_element_type=F32))
    pe = jnp.dot(p_bf16, w_proj_ref[...], preferred_element_type=F32)
    return x1 + pe * gate


def _attn_out_kernel(o_ref, x_ref, p_ref, g_ref, wo_ref, wp_ref, wg_ref, out_ref):
    o = jnp.concatenate([o_ref[0, c] for c in range(N_PAIRS)], axis=1)
    y = jnp.dot(o, wo_ref[...], preferred_element_type=F32)
    out_ref[0] = _layer_tail(y, x_ref[0], p_ref[0, 0].astype(BF16), g_ref[...], wp_ref, wg_ref)


def _attn_out_proj(og, x, p, layer, post_g, w_out, w_proj, w_gate):
    B, S, _ = x.shape
    tm = 512
    return pl.pallas_call(
        _attn_out_kernel,
        grid=(B, S // tm),
        in_specs=[pl.BlockSpec((1, N_PAIRS, tm, LANES), lambda b, i: (b, 0, i, 0)),
                  pl.BlockSpec((1, tm, D_MODEL), lambda b, i: (b, i, 0)),
                  pl.BlockSpec((1, 1, tm, PLE_DIM), lambda b, i: (layer, b, i, 0)),
                  _const_spec((1, D_MODEL)),
                  _const_spec((ATTN_WIDTH, D_MODEL)),
                  _const_spec((PLE_DIM, D_MODEL)),
                  _const_spec((D_MODEL, D_MODEL))],
        out_specs=pl.BlockSpec((1, tm, D_MODEL), lambda b, i: (b, i, 0)),
        out_shape=jax.ShapeDtypeStruct((B, S, D_MODEL), F32),
        compiler_params=pltpu.CompilerParams(
            dimension_semantics=("parallel", "parallel"), vmem_limit_bytes=VMEM_LIMIT),
        name="attn_out_proj",
    )(og, x, p, post_g, w_out, w_proj, w_gate)


def _conv_kernel(xh_ref, x_ref, p_ref, pre_g_ref, w_in_ref, dw_w_ref, dw_b_ref, ln_g_ref,
                 ln_b_ref, w_out_ref, post_g_ref, wp_ref, wg_ref, out_ref, u_scr, cv_scr, *, tm):
    C = D_MODEL
    x = x_ref[0]
    xa = jnp.concatenate([xh_ref[0], x], axis=0)
    h = _rmsnorm(xa, pre_g_ref[...]).astype(BF16)
    ab = jnp.dot(h, w_in_ref[:, :2 * C], preferred_element_type=F32)
    u = ab[:, :C] * _sigmoid(ab[:, C:])
    row = lax.broadcasted_iota(jnp.int32, u.shape, 0)
    u = jnp.where((row >= CONV_HALO) | (pl.program_id(1) > 0), u, 0.0)
    for c in range(C // LANES):
        u_scr[c] = u[:, c * LANES:(c + 1) * LANES]

    RB = 128
    first = CONV_HALO - (CONV_WIDTH - 1)

    def conv_rows(i, carry):
        r0 = pl.multiple_of(i * RB, RB)
        for c in range(C // LANES):
            lanes = slice(c * LANES, (c + 1) * LANES)
            win = u_scr.at[c, pl.ds(r0, RB + CONV_HALO), :]
            acc = jnp.broadcast_to(dw_b_ref[:, lanes], (RB, LANES))
            for w in range(CONV_WIDTH):
                acc = acc + win[pl.ds(first + w, RB), :] * dw_w_ref[pl.ds(w, 1), lanes]
            cv_scr[c, pl.ds(r0, RB), :] = acc
        return carry

    lax.fori_loop(0, tm // RB, conv_rows, 0)

    z = jnp.dot(h[CONV_HALO:], w_in_ref[:, 2 * C:], preferred_element_type=F32)
    cv = jnp.concatenate([cv_scr[c] for c in range(C // LANES)], axis=1)
    mu = jnp.mean(cv, axis=-1, keepdims=True)
    cen = cv - mu
    var = jnp.mean(cen * cen, axis=-1, keepdims=True)
    y = cen * lax.rsqrt(var + LN_EPS) * ln_g_ref[...] + ln_b_ref[...]
    y = y * _sigmoid(y)
    yg = (y * (z * _sigmoid(z))).astype(BF16)
    out = jnp.dot(yg, w_out_ref[...], preferred_element_type=F32)
    out_ref[0] = _layer_tail(out, x, p_ref[0, 0].astype(BF16), post_g_ref[...], wp_ref, wg_ref)


def _conv_layer(x, p, layer, pre_g, w_in, dw_w, dw_b, ln_g, ln_b, w_out, post_g, w_proj, w_gate):
    B, S, C = x.shape
    tm = 512
    hb = tm // CONV_HALO
    return pl.pallas_call(
        functools.partial(_conv_kernel, tm=tm),
        grid=(B, S // tm),
        in_specs=[pl.BlockSpec((1, CONV_HALO, C), lambda b, i: (b, jnp.maximum(i * hb - 1, 0), 0)),
                  pl.BlockSpec((1, tm, C), lambda b, i: (b, i, 0)),
                  pl.BlockSpec((1, 1, tm, PLE_DIM), lambda b, i: (layer, b, i, 0)),
                  _const_spec((1, C)),
                  _const_spec((C, 3 * C)),
                  _const_spec((CONV_WIDTH, C)),
                  _const_spec((1, C)),
                  _const_spec((1, C)),
                  _const_spec((1, C)),
                  _const_spec((C, C)),
                  _const_spec((1, C)),
                  _const_spec((PLE_DIM, C)),
                  _const_spec((C, C))],
        out_specs=pl.BlockSpec((1, tm, C), lambda b, i: (b, i, 0)),
        out_shape=jax.ShapeDtypeStruct((B, S, C), F32),
        scratch_shapes=[pltpu.VMEM((C // LANES, tm + CONV_HALO, LANES), F32),
                        pltpu.VMEM((C // LANES, tm, LANES), F32)],
        compiler_params=pltpu.CompilerParams(
            dimension_semantics=("parallel", "parallel"), vmem_limit_bytes=VMEM_LIMIT),
        name="conv_layer",
    )(x, x, p, pre_g, w_in, dw_w, dw_b, ln_g, ln_b, w_out, post_g, w_proj, w_gate)


def kernel(x, p, positions, pre_norm_g, post_norm_g, attn_w_in, attn_w_out, conv_w_in, conv_dw_w,
           conv_dw_b, conv_ln_g, conv_ln_b, conv_w_out, ple_w_proj, ple_w_gate):
    depth = p.shape[0]
    cos, sin = _rope_tables(positions)
    for i in range(depth):
        j = i // 2
        pre_g = pre_norm_g[i][None, :]
        post_g = post_norm_g[i][None, :]
        w_proj = ple_w_proj[i].astype(BF16)
        w_gate = ple_w_gate[i].astype(BF16)
        if i % 2 == 0:
            outs = _attn_in_proj(x, cos, sin, pre_g, attn_w_in[j].astype(BF16))
            og = _attention(outs[:9], outs[9])
            x = _attn_out_proj(og, x, p, i, post_g, attn_w_out[j].astype(BF16), w_proj, w_gate)
        else:
            x = _conv_layer(x, p, i, pre_g, conv_w_in[j].astype(BF16), conv_dw_w[j],
                            conv_dw_b[j][None, :], conv_ln_g[j][None, :], conv_ln_b[j][None, :],
                            conv_w_out[j].astype(BF16), post_g, w_proj, w_gate)
    return x
```

```python
import functools

import jax
import jax.numpy as jnp
import numpy as np
from jax import lax
from jax.experimental import pallas as pl
from jax.experimental.pallas import tpu as pltpu

D_MODEL = 1024
PLE_DIM = 256
HEAD_DIM = 64
N_HEADS = 16
N_PAIRS = N_HEADS // 2
ATTN_WIDTH = N_HEADS * HEAD_DIM
DILATIONS = (1, 4, 16)
N_BACK = 128
N_GROUPS = 3
ROPE_THETA = 10000.0
CONV_WIDTH = 31
CONV_HALO = 32
RMS_EPS = 1e-6
LN_EPS = 1e-5
NEG_INF = -1e30
LANES = 128
QK_SCALE = HEAD_DIM ** -0.5
PITCH16 = 20

F32 = jnp.float32
BF16 = jnp.bfloat16

VMEM_LIMIT = 56 * 1024 * 1024


def _const_spec(shape):
    return pl.BlockSpec(shape, lambda *_: (0,) * len(shape), pipeline_mode=pl.Buffered(1))


def _rmsnorm(x, g):
    return x * lax.rsqrt(jnp.mean(x * x, axis=-1, keepdims=True) + RMS_EPS) * g


def _sigmoid(x):
    return 1.0 / (1.0 + jnp.exp(-x))


N_FREQ = HEAD_DIM // 2
POS_PER_ROW = LANES // N_FREQ


def _rope_kernel(pos_ref, invf_ref, cos_ref, sin_ref):
    pos = pos_ref[0].astype(F32)
    rows = pos.shape[0]
    lane = lax.broadcasted_iota(jnp.int32, (rows, LANES), 1)
    pos_l = pos[:, POS_PER_ROW - 1:POS_PER_ROW]
    for j in range(POS_PER_ROW - 2, -1, -1):
        pos_l = jnp.where(lane < (j + 1) * N_FREQ, pos[:, j:j + 1], pos_l)
    ang = pos_l * invf_ref[...]
    tables = (jnp.cos(ang), jnp.sin(ang))
    is_lo = lane % HEAD_DIM < N_FREQ

    def spread(tab, j):
        base = tab if j == 0 else pltpu.roll(tab, LANES - j * N_FREQ, axis=1)
        out = base
        for g in range(1, POS_PER_ROW):
            out = jnp.where(lane >= g * N_FREQ, pltpu.roll(base, g * N_FREQ, axis=1), out)
        return out

    for j in range(POS_PER_ROW):
        rows_j = pl.ds(j, rows, stride=POS_PER_ROW)
        cos_ref[0, rows_j, :] = spread(tables[0], j)
        sin_j = spread(tables[1], j)
        sin_ref[0, rows_j, :] = jnp.where(is_lo, -sin_j, sin_j)


def _rope_tables(positions):
    B, S = positions.shape
    rows = S // POS_PER_ROW
    inv_freq = 1.0 / (ROPE_THETA ** (jnp.arange(0, HEAD_DIM, 2, dtype=F32) / HEAD_DIM))
    invf = jnp.tile(inv_freq, POS_PER_ROW)[None, :]
    return pl.pallas_call(
        _rope_kernel,
        grid=(B,),
        in_specs=[pl.BlockSpec((1, rows, POS_PER_ROW), lambda b: (b, 0, 0)),
                  pl.BlockSpec((1, LANES), lambda b: (0, 0))],
        out_specs=[pl.BlockSpec((1, S, LANES), lambda b: (b, 0, 0))] * 2,
        out_shape=[jax.ShapeDtypeStruct((B, S, LANES), F32)] * 2,
        name="rope_tables",
    )(positions.reshape(B, rows, POS_PER_ROW), invf)


def _deinterleave(ref, lead, d, rows):
    parts = [ref[lead + (pl.ds(r, rows // d, stride=d), slice(None))] for r in range(d)]
    return jnp.concatenate(parts, axis=0)


def _attn_in_kernel(x_ref, cos_ref, sin_ref, g_ref, w_ref, *refs, tm):
    out_refs = refs[:10]
    hs_ref = refs[10]
    qkv_refs = [out_refs[3 * g:3 * g + 3] for g in range(N_GROUPS)]
    zg_ref = out_refs[9]

    hn = _rmsnorm(x_ref[0], g_ref[...])
    for c in range(N_PAIRS):
        hs_ref[c] = hn[:, c * LANES:(c + 1) * LANES]
    h0 = hn.astype(BF16)
    lane = lax.broadcasted_iota(jnp.int32, (tm, LANES), 1)
    is_lo = lane % HEAD_DIM < N_FREQ

    for g, d in enumerate(DILATIONS):
        if d == 1:
            hg, cg, sg = h0, cos_ref[0], sin_ref[0]
        else:
            hg = jnp.concatenate(
                [_deinterleave(hs_ref, (c,), d, tm) for c in range(N_PAIRS)], axis=1).astype(BF16)
            cg = _deinterleave(cos_ref, (0,), d, tm)
            sg = _deinterleave(sin_ref, (0,), d, tm)
        for part in range(3):
            col0 = (part * N_GROUPS + g) * ATTN_WIDTH
            for cc in range(ATTN_WIDTH // 256):
                acc = jnp.dot(hg, w_ref[:, col0 + cc * 256:col0 + (cc + 1) * 256],
                              preferred_element_type=F32)
                for half in range(2):
                    t = acc[:, half * LANES:(half + 1) * LANES]
                    if part < 2:
                        rot = jnp.where(is_lo, pltpu.roll(t, LANES - N_FREQ, axis=1),
                                        pltpu.roll(t, N_FREQ, axis=1))
                        t = t * cg + rot * sg
                    if part == 0:
                        t = t * QK_SCALE
                    qkv_refs[g][part][0, cc * 2 + half] = (
                        t.reshape(d, tm // d, LANES).astype(BF16))
    zc0 = 3 * N_GROUPS * ATTN_WIDTH
    for cc in range(ATTN_WIDTH // 256):
        z = jnp.dot(h0, w_ref[:, zc0 + cc * 256:zc0 + (cc + 1) * 256],
                    preferred_element_type=F32)
        zg = z * _sigmoid(z)
        for half in range(2):
            zg_ref[0, cc * 2 + half] = zg[:, half * LANES:(half + 1) * LANES].astype(BF16)


def _attn_in_proj(x, cos, sin, g, w):
    B, S, _ = x.shape
    tm = 256
    ncols = w.shape[1]
    out_shape, out_specs = [], []
    for d in DILATIONS:
        L = S // d
        for _ in range(3):
            out_shape.append(jax.ShapeDtypeStruct((B, N_PAIRS, d, L, LANES), BF16))
            out_specs.append(pl.BlockSpec((1, N_PAIRS, d, tm // d, LANES),
                                          lambda b, i: (b, 0, 0, i, 0)))
    out_shape.append(jax.ShapeDtypeStruct((B, N_PAIRS, S, LANES), BF16))
    out_specs.append(pl.BlockSpec((1, N_PAIRS, tm, LANES), lambda b, i: (b, 0, i, 0)))
    return pl.pallas_call(
        functools.partial(_attn_in_kernel, tm=tm),
        grid=(B, S // tm),
        in_specs=[pl.BlockSpec((1, tm, D_MODEL), lambda b, i: (b, i, 0)),
                  pl.BlockSpec((1, tm, LANES), lambda b, i: (b, i, 0)),
                  pl.BlockSpec((1, tm, LANES), lambda b, i: (b, i, 0)),
                  _const_spec((1, D_MODEL)),
                  _const_spec((D_MODEL, ncols))],
        out_specs=out_specs,
        out_shape=out_shape,
        scratch_shapes=[pltpu.VMEM((N_PAIRS, tm, LANES), F32)],
        compiler_params=pltpu.CompilerParams(
            dimension_semantics=("parallel", "parallel"), vmem_limit_bytes=VMEM_LIMIT),
        name="attn_in_proj",
    )(x, cos, sin, g, w)


def _attn_kernel(*refs, S, unroll):
    qkv_refs = [refs[3 * g:3 * g + 3] for g in range(N_GROUPS)]
    zg_ref, bias_ref, o_ref = refs[9:12]
    out01_scr, lse01_scr, out2_scr, lse2_scr, p_scr, m_scr = refs[12:18]
    T = N_BACK
    n_steps = (S // T) // unroll

    lane = lax.broadcasted_iota(jnp.int32, (T, LANES), 1)
    v_is_h0 = lane < HEAD_DIM
    q_is_h0 = v_is_h0
    ones = jnp.ones((2 * T, LANES), BF16)

    def block_coords(g, idx):
        nb = (S // DILATIONS[g]) // T
        r = idx // nb
        n = idx % nb
        start = pl.multiple_of(n * T, T)
        kstart = pl.multiple_of(jnp.maximum(start - T, 0), T)
        return r, n, start, kstart

    def stage_a(g, step, slot):
        q_ref, k_ref, _ = qkv_refs[g]
        for u in range(unroll):
            r, n, start, kstart = block_coords(g, step * unroll + u)
            q = q_ref[0, 0, r, pl.ds(start, T), :]
            k = k_ref[0, 0, r, pl.ds(kstart, 2 * T), :]
            zero = jnp.zeros_like(q)
            qq = jnp.concatenate(
                [jnp.where(q_is_h0, q, zero), jnp.where(q_is_h0, zero, q)], axis=0)
            s = lax.dot_general(qq, k, (((1,), (1,)), ((), ())), preferred_element_type=F32)
            s = s + bias_ref[jnp.where(n == 0, 1, 0)]
            m = jnp.max(s, axis=-1, keepdims=True)
            p_scr[slot, u] = jnp.exp(s - m).astype(BF16)
            m_scr[slot, u] = jnp.where(v_is_h0, m[:T], m[T:])

    def stage_b(g, step, slot):
        d = DILATIONS[g]
        v_ref = qkv_refs[g][2]
        for u in range(unroll):
            r, n, start, kstart = block_coords(g, step * unroll + u)
            v = v_ref[0, 0, r, pl.ds(kstart, 2 * T), :]
            res = jnp.dot(p_scr[slot, u], jnp.concatenate([v, ones], axis=1),
                          preferred_element_type=F32)
            acc = jnp.where(v_is_h0, res[:T, :LANES], res[T:, :LANES])
            l = jnp.where(v_is_h0, res[:T, LANES:], res[T:, LANES:])
            out = acc * (1.0 / l)
            lse = m_scr[slot, u] + jnp.log(l)
            if d == 1:
                out01_scr[0, pl.ds(start, T), :] = out
                lse01_scr[0, pl.ds(start, T), :] = lse
            elif d == 4:
                rows = pl.ds(start * d + r, T, stride=d)
                out01_scr[1, rows, :] = out
                lse01_scr[1, rows, :] = lse
            else:
                rows = pl.ds(start * PITCH16 + r, T, stride=PITCH16)
                out2_scr[rows, :] = out
                lse2_scr[rows, :] = lse

    stage_a(0, 0, 0)
    for g in range(N_GROUPS):
        base = g * n_steps

        def steady(i, carry, g=g, base=base):
            slot = (base + i) % 2
            stage_b(g, i, slot)
            stage_a(g, i + 1, 1 - slot)
            return carry

        lax.fori_loop(0, n_steps - 1, steady, 0)
        last_slot = (base + n_steps - 1) % 2
        stage_b(g, n_steps - 1, last_slot)
        if g + 1 < N_GROUPS:
            stage_a(g + 1, 0, 1 - last_slot)

    def combine(c, carry):
        rows = pl.ds(pl.multiple_of(c * T, T), T)
        base2 = pl.multiple_of(c * (T // 16) * PITCH16, 8)
        win2 = [pl.ds(base2 + j * PITCH16, 16) for j in range(T // 16)]
        lses = [lse01_scr[0, rows, :], lse01_scr[1, rows, :],
                jnp.concatenate([lse2_scr[w, :] for w in win2], axis=0)]
        outs = [out01_scr[0, rows, :], out01_scr[1, rows, :],
                jnp.concatenate([out2_scr[w, :] for w in win2], axis=0)]
        mx = jnp.maximum(jnp.maximum(lses[0], lses[1]), lses[2])
        num = jnp.zeros((T, LANES), F32)
        den = jnp.zeros((T, LANES), F32)
        for g in range(N_GROUPS):
            w = jnp.exp(lses[g] - mx)
            num = num + w * outs[g]
            den = den + w
        o_ref[0, 0, rows, :] = (num * (1.0 / den) * zg_ref[0, 0, rows, :].astype(F32)).astype(BF16)
        return carry

    lax.fori_loop(0, S // T, combine, 0, unroll=4)


def _band_bias():
    T = N_BACK
    row = np.arange(T)[:, None]
    col = np.arange(2 * T)[None, :]
    tables = []
    for off in (T, 0):
        dist = off + row - col
        b = np.where((dist >= 0) & (dist <= T), 0.0, NEG_INF).astype(np.float32)
        tables.append(np.concatenate([b, b], axis=0))
    return jnp.asarray(np.stack(tables))


def _attention(qkv, zg):
    B, _, S, _ = zg.shape
    T = N_BACK
    unroll = 16
    in_specs = []
    for d in DILATIONS:
        L = S // d
        in_specs += [pl.BlockSpec((1, 1, d, L, LANES), lambda b, p: (b, p, 0, 0, 0))] * 3
    in_specs.append(pl.BlockSpec((1, 1, S, LANES), lambda b, p: (b, p, 0, 0)))
    in_specs.append(_const_spec((2, 2 * T, 2 * T)))
    rows16 = S // 16 * PITCH16
    return pl.pallas_call(
        functools.partial(_attn_kernel, S=S, unroll=unroll),
        grid=(B, N_PAIRS),
        in_specs=in_specs,
        out_specs=pl.BlockSpec((1, 1, S, LANES), lambda b, p: (b, p, 0, 0)),
        out_shape=jax.ShapeDtypeStruct((B, N_PAIRS, S, LANES), BF16),
        scratch_shapes=[pltpu.VMEM((2, S, LANES), F32),
                        pltpu.VMEM((2, S, LANES), F32),
                        pltpu.VMEM((rows16, LANES), F32),
                        pltpu.VMEM((rows16, LANES), F32),
                        pltpu.VMEM((2, unroll, 2 * T, 2 * T), BF16),
                        pltpu.VMEM((2, unroll, T, LANES), F32)],
        compiler_params=pltpu.CompilerParams(
            dimension_semantics=("parallel", "parallel"), vmem_limit_bytes=VMEM_LIMIT),
        name="dilated_attention",
    )(*qkv, zg, _band_bias())


def _layer_tail(y, x, p_bf16, post_g, w_proj_ref, w_gate_ref):
    x1 = x + _rmsnorm(y, post_g)
    gate = _sigmoid(jnp.dot(x1.astype(BF16), w_gate_ref[...], preferred_element_type=F32))
    pe = jnp.dot(p_bf16, w_proj_ref[...], preferred_element_type=F32)
    return x1 + pe * gate


def _attn_out_kernel(o_ref, x_ref, p_ref, g_ref, wo_ref, wp_ref, wg_ref, out_ref):
    o = jnp.concatenate([o_ref[0, c] for c in range(N_PAIRS)], axis=1)
    y = jnp.dot(o, wo_ref[...], preferred_element_type=F32)
    out_ref[0] = _layer_tail(y, x_ref[0], p_ref[0, 0].astype(BF16), g_ref[...], wp_ref, wg_ref)


def _attn_out_proj(og, x, p, layer, post_g, w_out, w_proj, w_gate):
    B, S, _ = x.shape
    tm = 1024
    return pl.pallas_call(
        _attn_out_kernel,
        grid=(B, S // tm),
        in_specs=[pl.BlockSpec((1, N_PAIRS, tm, LANES), lambda b, i: (b, 0, i, 0)),
                  pl.BlockSpec((1, tm, D_MODEL), lambda b, i: (b, i, 0)),
                  pl.BlockSpec((1, 1, tm, PLE_DIM), lambda b, i: (layer, b, i, 0)),
                  _const_spec((1, D_MODEL)),
                  _const_spec((ATTN_WIDTH, D_MODEL)),
                  _const_spec((PLE_DIM, D_MODEL)),
                  _const_spec((D_MODEL, D_MODEL))],
        out_specs=pl.BlockSpec((1, tm, D_MODEL), lambda b, i: (b, i, 0)),
        out_shape=jax.ShapeDtypeStruct((B, S, D_MODEL), F32),
        compiler_params=pltpu.CompilerParams(
            dimension_semantics=("parallel", "parallel"), vmem_limit_bytes=VMEM_LIMIT),
        name="attn_out_proj",
    )(og, x, p, post_g, w_out, w_proj, w_gate)


def _conv_kernel(xh_ref, x_ref, p_ref, pre_g_ref, w_in_ref, dw_w_ref, dw_b_ref, ln_g_ref,
                 ln_b_ref, w_out_ref, post_g_ref, wp_ref, wg_ref, out_ref, u_scr, cv_scr, *, tm):
    C = D_MODEL
    x = x_ref[0]
    xa = jnp.concatenate([xh_ref[0], x], axis=0)
    h = _rmsnorm(xa, pre_g_ref[...]).astype(BF16)
    ab = jnp.dot(h, w_in_ref[:, :2 * C], preferred_element_type=F32)
    u = ab[:, :C] * _sigmoid(ab[:, C:])
    row = lax.broadcasted_iota(jnp.int32, u.shape, 0)
    u = jnp.where((row >= CONV_HALO) | (pl.program_id(1) > 0), u, 0.0)
    for c in range(C // LANES):
        u_scr[c] = u[:, c * LANES:(c + 1) * LANES]

    RB = 128
    first = CONV_HALO - (CONV_WIDTH - 1)

    def conv_rows(i, carry):
        r0 = pl.multiple_of(i * RB, RB)
        for c in range(C // LANES):
            lanes = slice(c * LANES, (c + 1) * LANES)
            win = u_scr.at[c, pl.ds(r0, RB + CONV_HALO), :]
            acc = jnp.broadcast_to(dw_b_ref[:, lanes], (RB, LANES))
            for w in range(CONV_WIDTH):
                acc = acc + win[pl.ds(first + w, RB), :] * dw_w_ref[pl.ds(w, 1), lanes]
            cv_scr[c, pl.ds(r0, RB), :] = acc
        return carry

    lax.fori_loop(0, tm // RB, conv_rows, 0)

    z = jnp.dot(h[CONV_HALO:], w_in_ref[:, 2 * C:], preferred_element_type=F32)
    cv = jnp.concatenate([cv_scr[c] for c in range(C // LANES)], axis=1)
    mu = jnp.mean(cv, axis=-1, keepdims=True)
    cen = cv - mu
    var = jnp.mean(cen * cen, axis=-1, keepdims=True)
    y = cen * lax.rsqrt(var + LN_EPS) * ln_g_ref[...] + ln_b_ref[...]
    y = y * _sigmoid(y)
    yg = (y * (z * _sigmoid(z))).astype(BF16)
    out = jnp.dot(yg, w_out_ref[...], preferred_element_type=F32)
    out_ref[0] = _layer_tail(out, x, p_ref[0, 0].astype(BF16), post_g_ref[...], wp_ref, wg_ref)


def _conv_layer(x, p, layer, pre_g, w_in, dw_w, dw_b, ln_g, ln_b, w_out, post_g, w_proj, w_gate):
    B, S, C = x.shape
    tm = 512
    hb = tm // CONV_HALO
    return pl.pallas_call(
        functools.partial(_conv_kernel, tm=tm),
        grid=(B, S // tm),
        in_specs=[pl.BlockSpec((1, CONV_HALO, C), lambda b, i: (b, jnp.maximum(i * hb - 1, 0), 0)),
                  pl.BlockSpec((1, tm, C), lambda b, i: (b, i, 0)),
                  pl.BlockSpec((1, 1, tm, PLE_DIM), lambda b, i: (layer, b, i, 0)),
                  _const_spec((1, C)),
                  _const_spec((C, 3 * C)),
                  _const_spec((CONV_WIDTH, C)),
                  _const_spec((1, C)),
                  _const_spec((1, C)),
                  _const_spec((1, C)),
                  _const_spec((C, C)),
                  _const_spec((1, C)),
                  _const_spec((PLE_DIM, C)),
                  _const_spec((C, C))],
        out_specs=pl.BlockSpec((1, tm, C), lambda b, i: (b, i, 0)),
        out_shape=jax.ShapeDtypeStruct((B, S, C), F32),
        scratch_shapes=[pltpu.VMEM((C // LANES, tm + CONV_HALO, LANES), F32),
                        pltpu.VMEM((C // LANES, tm, LANES), F32)],
        compiler_params=pltpu.CompilerParams(
            dimension_semantics=("parallel", "parallel"), vmem_limit_bytes=VMEM_LIMIT),
        name="conv_layer",
    )(x, x, p, pre_g, w_in, dw_w, dw_b, ln_g, ln_b, w_out, post_g, w_proj, w_gate)


def kernel(x, p, positions, pre_norm_g, post_norm_g, attn_w_in, attn_w_out, conv_w_in, conv_dw_w,
           conv_dw_b, conv_ln_g, conv_ln_b, conv_w_out, ple_w_proj, ple_w_gate):
    depth = p.shape[0]
    cos, sin = _rope_tables(positions)
    for i in range(depth):
        j = i // 2
        pre_g = pre_norm_g[i][None, :]
        post_g = post_norm_g[i][None, :]
        w_proj = ple_w_proj[i].astype(BF16)
        w_gate = ple_w_gate[i].astype(BF16)
        if i % 2 == 0:
            outs = _attn_in_proj(x, cos, sin, pre_g, attn_w_in[j].astype(BF16))
            og = _attention(outs[:9], outs[9])
            x = _attn_out_proj(og, x, p, i, post_g, attn_w_out[j].astype(BF16), w_proj, w_gate)
        else:
            x = _conv_layer(x, p, i, pre_g, conv_w_in[j].astype(BF16), conv_dw_w[j],
                            conv_dw_b[j][None, :], conv_ln_g[j][None, :], conv_ln_b[j][None, :],
                            conv_w_out[j].astype(BF16), post_g, w_proj, w_gate)
    return x
```
